```python
import jax
import jax.numpy as jnp
from jax import lax
import numpy as np

D_MODEL = 1024
BATCH = 8
SEQ = 4096
DEPTH = 1

EPS = 1e-6
N_MEM = 256
Q_BLOCK = 128

A_HEADS = 8
A_HEAD_DIM = 64
A_WIDTH = A_HEADS * A_HEAD_DIM
DILATED_BRANCHES = ((128, 1), (512, 4), (2048, 16))

B_HEADS = 4
B_NOPE = 128
B_ROPE = 64
B_V = 128
B_WIDTH = B_HEADS * B_V
Q_LORA = 384
KV_LORA = 256
ROPE_THETA = 10000.0

MIX_WIDTH = A_WIDTH + B_WIDTH
IN_SPLITS = (A_WIDTH, A_WIDTH, A_WIDTH, Q_LORA, KV_LORA, B_ROPE)
D_IN = sum(IN_SPLITS)
SPLIT_POINTS = tuple(sum(IN_SPLITS[:i + 1]) for i in range(len(IN_SPLITS) - 1))

M_HEADS = 4
M_HEAD_DIM = 128
M_WIDTH = M_HEADS * M_HEAD_DIM

D_FF = -(-8 * D_MODEL // (3 * 256)) * 256

kernel_name = 'hymba_dilated_mla_memory_encoder'


def rms_norm(x, g):
    xf = x.astype(jnp.float32)
    y = xf * lax.rsqrt(jnp.mean(xf * xf, axis=-1, keepdims=True) + EPS)
    return (y * g.astype(jnp.float32)).astype(x.dtype)


def alibi_slopes(n):
    return 2.0 ** (-8.0 * jnp.arange(1, n + 1, dtype=jnp.float32) / n)


def apply_rope(t, cos, sin):
    tf = t.astype(jnp.float32)
    t1, t2 = jnp.split(tf, 2, axis=-1)
    return jnp.concatenate([t1 * cos - t2 * sin, t2 * cos + t1 * sin], axis=-1).astype(t.dtype)


def dilated_attention(q, k, v, positions):
    S = q.shape[1]
    scale = A_HEAD_DIM ** -0.5
    slopes = alibi_slopes(A_HEADS)

    def block(start):
        t = start + jnp.arange(Q_BLOCK)
        qb = lax.dynamic_slice_in_dim(q, start, Q_BLOCK, axis=1)
        pq = lax.dynamic_slice_in_dim(positions, start, Q_BLOCK, axis=1)
        outs, lses = [], []
        for window, dil in DILATED_BRANCHES:
            n = window // (2 * dil)
            offs = jnp.arange(-n, n + 1) * dil
            idx = t[:, None] + offs[None, :]
            valid = (idx >= 0) & (idx < S)
            idx = jnp.clip(idx, 0, S - 1)
            kg = k[:, idx]
            vg = v[:, idx]
            pk = positions[:, idx]
            dist = jnp.abs(pq[:, :, None] - pk).astype(jnp.float32)
            s = jnp.einsum('bqhd,bqkhd->bhqk', qb, kg).astype(jnp.float32) * scale
            s = s - slopes[None, :, None, None] * dist[:, None]
            s = jnp.where(valid[None, None], s, -jnp.inf)
            lse = jax.nn.logsumexp(s, axis=-1)
            p = jnp.exp(s - lse[..., None]).astype(v.dtype)
            outs.append(jnp.einsum('bhqk,bqkhd->bqhd', p, vg))
            lses.append(lse)
        alpha = jax.nn.softmax(jnp.stack(lses), axis=0)
        alpha = jnp.transpose(alpha, (0, 1, 3, 2))[..., None]
        o = jnp.sum(alpha * jnp.stack(outs).astype(jnp.float32), axis=0)
        return o.astype(q.dtype)

    starts = jnp.arange(S // Q_BLOCK) * Q_BLOCK
    o = lax.map(block, starts)
    return jnp.moveaxis(o, 0, 1).reshape(q.shape)


def dense_attention(q, k, v, scale):
    S = q.shape[1]

    def block(start):
        qb = lax.dynamic_slice_in_dim(q, start, Q_BLOCK, axis=1)
        s = jnp.einsum('bqhd,bkhd->bhqk', qb, k).astype(jnp.float32) * scale
        p = jax.nn.softmax(s, axis=-1).astype(v.dtype)
        return jnp.einsum('bhqk,bkhd->bqhd', p, v)

    starts = jnp.arange(S // Q_BLOCK) * Q_BLOCK
    o = lax.map(block, starts)
    return jnp.moveaxis(o, 0, 1).reshape(q.shape[:3] + (v.shape[-1],))


def setup_inputs(seed: int = 0) -> dict:
    key = jax.random.key(seed)
    ks = jax.random.split(key, 24)

    def dense(k, fan_in, fan_out):
        return jax.random.normal(k, (DEPTH, fan_in, fan_out), jnp.float32) * fan_in ** -0.5

    def gain(k, n):
        return 1.0 + 0.02 * jax.random.normal(k, (DEPTH, n), jnp.float32)

    x = jax.random.normal(ks[0], (BATCH, SEQ, D_MODEL), jnp.float32)
    mem = jax.random.normal(ks[1], (BATCH, N_MEM, D_MODEL), jnp.float32)
    offset = jax.random.randint(ks[2], (BATCH, 1), 0, 1024, dtype=jnp.int32)
    positions = jnp.arange(SEQ, dtype=jnp.int32)[None, :] + offset
    return {
        'x': x,
        'mem': mem,
        'positions': positions,
        'norm_mix': gain(ks[3], D_MODEL),
        'w_in': dense(ks[4], D_MODEL, D_IN),
        'q_norm': gain(ks[5], Q_LORA),
        'w_q_up': dense(ks[6], Q_LORA, B_HEADS * (B_NOPE + B_ROPE)),
        'kv_norm': gain(ks[7], KV_LORA),
        'w_kv_up': dense(ks[8], KV_LORA, B_HEADS * (B_NOPE + B_V)),
        'gout_a': gain(ks[9], A_WIDTH),
        'gout_b': gain(ks[10], B_WIDTH),
        'w_out': dense(ks[11], MIX_WIDTH, D_MODEL),
        'norm_mem_q': gain(ks[12], D_MODEL),
        'norm_mem_kv': gain(ks[13], D_MODEL),
        'w_mq': dense(ks[14], D_MODEL, M_WIDTH),
        'w_mkv': dense(ks[15], D_MODEL, 2 * M_WIDTH),
        'w_mo': dense(ks[16], M_WIDTH, D_MODEL),
        'norm_ffn': gain(ks[17], D_MODEL),
        'w_gate': dense(ks[18], D_MODEL, D_FF),
        'w_up': dense(ks[19], D_MODEL, D_FF),
        'w_down': dense(ks[20], D_FF, D_MODEL),
        'norm_final': 1.0 + 0.02 * jax.random.normal(ks[21], (D_MODEL,), jnp.float32),
    }


def reference(x, mem, positions, norm_mix, w_in, q_norm, w_q_up, kv_norm, w_kv_up,
              gout_a, gout_b, w_out, norm_mem_q, norm_mem_kv, w_mq, w_mkv, w_mo,
              norm_ffn, w_gate, w_up, w_down, norm_final):
    B, S = x.shape[0], x.shape[1]
    M = mem.shape[1]
    half = B_ROPE // 2
    inv_freq = ROPE_THETA ** (-jnp.arange(half, dtype=jnp.float32) / half)
    ang = positions.astype(jnp.float32)[..., None] * inv_freq
    cos, sin = jnp.cos(ang), jnp.sin(ang)

    for l in range(DEPTH):
        h = rms_norm(x, norm_mix[l])
        proj = h @ w_in[l]
        qa, ka, va, cq, ckv, kr = jnp.split(proj, SPLIT_POINTS, axis=-1)

        hs_a = (B, S, A_HEADS, A_HEAD_DIM)
        o_a = dilated_attention(qa.reshape(hs_a), ka.reshape(hs_a), va.reshape(hs_a), positions)
        o_a = o_a.reshape(B, S, A_WIDTH)

        qb = (rms_norm(cq, q_norm[l]) @ w_q_up[l]).reshape(B, S, B_HEADS, B_NOPE + B_ROPE)
        q_nope, q_pe = jnp.split(qb, [B_NOPE], axis=-1)
        q_pe = apply_rope(q_pe, cos[:, :, None, :], sin[:, :, None, :])
        kvb = (rms_norm(ckv, kv_norm[l]) @ w_kv_up[l]).reshape(B, S, B_HEADS, B_NOPE + B_V)
        k_nope, v_b = jnp.split(kvb, [B_NOPE], axis=-1)
        k_pe = apply_rope(kr, cos, sin)
        k_pe = jnp.broadcast_to(k_pe[:, :, None, :], (B, S, B_HEADS, B_ROPE))
        q_b = jnp.concatenate([q_nope, q_pe], axis=-1)
        k_b = jnp.concatenate([k_nope, k_pe], axis=-1)
        o_b = dense_attention(q_b, k_b, v_b, (B_NOPE + B_ROPE) ** -0.5).reshape(B, S, B_WIDTH)

        mixed = jnp.concatenate([rms_norm(o_a, gout_a[l]), rms_norm(o_b, gout_b[l])], axis=-1)
        x = x + mixed @ w_out[l]

        hq = rms_norm(x, norm_mem_q[l])
        mk = rms_norm(mem, norm_mem_kv[l])
        mq = (hq @ w_mq[l]).reshape(B, S, M_HEADS, M_HEAD_DIM)
        mkv = (mk @ w_mkv[l]).reshape(B, M, 2, M_HEADS, M_HEAD_DIM)
        mkk, mvv = mkv[:, :, 0], mkv[:, :, 1]
        s = jnp.einsum('bshd,bmhd->bhsm', mq, mkk).astype(jnp.float32) * M_HEAD_DIM ** -0.5
        p = jax.nn.softmax(s, axis=-1).astype(mvv.dtype)
        mo = jnp.einsum('bhsm,bmhd->bshd', p, mvv).reshape(B, S, M_WIDTH)
        x = x + mo @ w_mo[l]

        hf = rms_norm(x, norm_ffn[l])
        x = x + (jax.nn.silu(hf @ w_gate[l]) * (hf @ w_up[l])) @ w_down[l]

    return rms_norm(x, norm_final)
```

```python
import functools
import math

import jax
import jax.numpy as jnp
from jax import lax
from jax.experimental import pallas as pl
from jax.experimental.pallas import tpu as pltpu

F32 = jnp.float32
BF16 = jnp.bfloat16

EPS = 1e-6
D_MODEL = 1024
N_MEM = 256

A_HEADS = 8
A_HEAD_DIM = 64
A_WIDTH = A_HEADS * A_HEAD_DIM
DILATIONS = (1, 4, 16)
HALF_SPAN = 64
Q_TILE = 128
K_WIN = Q_TILE + 2 * HALF_SPAN

B_HEADS = 4
B_NOPE = 128
B_ROPE = 64
B_QK = B_NOPE + B_ROPE
B_V = 128
B_WIDTH = B_HEADS * B_V
Q_LORA = 384
KV_LORA = 256
ROPE_THETA = 10000.0

M_HEADS = 4
M_HEAD_DIM = 128
M_WIDTH = M_HEADS * M_HEAD_DIM

LOG2E = math.log2(math.e)
MASK_DIST = 1e30
LANES = 128
VMEM_LIMIT = 56 * 1024 * 1024


def _rms(x, g):
    x = x.astype(F32)
    return x * lax.rsqrt(jnp.mean(x * x, axis=-1, keepdims=True) + EPS) * g


def _dot(a, b):
    return jnp.dot(a, b, preferred_element_type=F32)


def _dot_nt(a, b):
    return lax.dot_general(a, b, (((1,), (1,)), ((), ())), preferred_element_type=F32)


def _params(*sem):
    return pltpu.CompilerParams(dimension_semantics=sem, vmem_limit_bytes=VMEM_LIMIT)


def _mem_kv_kernel(mem_ref, g_ref, w_ref, o_ref):
    o_ref[0] = _dot(_rms(mem_ref[0], g_ref[...]).astype(BF16), w_ref[...]).astype(BF16)


def _mem_kv(mem, g, w):
    B, M, D = mem.shape
    N = w.shape[1]
    return pl.pallas_call(
        _mem_kv_kernel,
        grid=(B,),
        in_specs=[
            pl.BlockSpec((1, M, D), lambda b: (b, 0, 0)),
            pl.BlockSpec((1, D), lambda b: (0, 0)),
            pl.BlockSpec((D, N), lambda b: (0, 0)),
        ],
        out_specs=pl.BlockSpec((1, M, N), lambda b: (b, 0, 0)),
        out_shape=jax.ShapeDtypeStruct((B, M, N), BF16),
        compiler_params=_params("arbitrary"),
        name="mem_kv",
    )(mem, g, w)


_C_QA, _C_KA, _C_VA, _C_CQ, _C_CKV, _C_KR, _C_END = 0, 512, 1024, 1536, 1920, 2176, 2304
_QA_SCALE = A_HEAD_DIM ** -0.5 * LOG2E
_QB_SCALE = B_QK ** -0.5 * LOG2E
_MQ_SCALE = M_HEAD_DIM ** -0.5 * LOG2E


def _in_proj_kernel(x_ref, pos_ref, invf_ref, nmix_ref, win_ref, qn_ref, wq_ref, kvn_ref, wkv_ref,
                    qa_ref, ka_ref, va_ref, qb_ref, kb_ref, vb_ref):
    h = _rms(x_ref[0], nmix_ref[...]).astype(BF16)

    def seg(a, b):
        return _dot(h, win_ref[:, a:b])

    qa_ref[0] = (seg(_C_QA, _C_KA) * _QA_SCALE).astype(BF16)
    ka_ref[0] = seg(_C_KA, _C_VA).astype(BF16)
    va_ref[0] = seg(_C_VA, _C_CQ).astype(BF16)
    cq = seg(_C_CQ, _C_CKV)
    ckv = seg(_C_CKV, _C_KR)
    kr2 = seg(_C_KR, _C_END)

    ang = pos_ref[0].astype(F32) * invf_ref[...]
    cos4 = jnp.cos(ang)
    sin4 = jnp.sin(ang)
    cos8 = jnp.concatenate([cos4, cos4], axis=1)
    sin8 = jnp.concatenate([sin4, sin4], axis=1)

    qb = _dot(_rms(cq, qn_ref[...]).astype(BF16), wq_ref[...])
    n0 = B_HEADS * B_NOPE
    n1 = n0 + B_HEADS * B_ROPE
    q_pe = (qb[:, n0:n1] * cos8 + qb[:, n1:] * sin8) * _QB_SCALE

    lane = lax.broadcasted_iota(jnp.int32, kr2.shape, 1)
    t = kr2 * jnp.where(lane < B_ROPE, cos4, sin4)
    k_pe = (t[:, :B_ROPE] + t[:, B_ROPE:]).astype(BF16)

    kvb = _dot(_rms(ckv, kvn_ref[...]).astype(BF16), wkv_ref[...])
    for hd in range(B_HEADS):
        qb_ref[0, hd, :, 0:B_NOPE] = (qb[:, hd * B_NOPE:(hd + 1) * B_NOPE] * _QB_SCALE).astype(BF16)
        qb_ref[0, hd, :, B_NOPE:B_QK] = q_pe[:, hd * B_ROPE:(hd + 1) * B_ROPE].astype(BF16)
        c0 = hd * (B_NOPE + B_V)
        kb_ref[0, hd, :, 0:B_NOPE] = kvb[:, c0:c0 + B_NOPE].astype(BF16)
        kb_ref[0, hd, :, B_NOPE:B_QK] = k_pe
        vb_ref[0, hd] = kvb[:, c0 + B_NOPE:c0 + B_NOPE + B_V].astype(BF16)


def _in_proj(x, pos_col, invf, nmix, win, qn, wq, kvn, wkv, tm):
    B, S, D = x.shape
    const = lambda shape: pl.BlockSpec(shape, lambda b, i: (0,) * len(shape))
    tok = lambda n: pl.BlockSpec((1, tm, n), lambda b, i: (b, i, 0))
    head = lambda n: pl.BlockSpec((1, B_HEADS, tm, n), lambda b, i: (b, 0, i, 0))
    return pl.pallas_call(
        _in_proj_kernel,
        grid=(B, S // tm),
        in_specs=[tok(D), tok(1), const((1, LANES)), const((1, D)), const(win.shape),
                  const((1, Q_LORA)), const(wq.shape), const((1, KV_LORA)), const(wkv.shape)],
        out_specs=[tok(A_WIDTH), tok(A_WIDTH), tok(A_WIDTH), head(B_QK), head(B_QK), head(B_V)],
        out_shape=[jax.ShapeDtypeStruct((B, S, A_WIDTH), BF16)] * 3
        + [jax.ShapeDtypeStruct((B, B_HEADS, S, B_QK), BF16)] * 2
        + [jax.ShapeDtypeStruct((B, B_HEADS, S, B_V), BF16)],
        compiler_params=_params("arbitrary", "arbitrary"),
        name="in_proj",
    )(x, pos_col, invf, nmix, win, qn, wq, kvn, wkv)


def _dilated_kernel(*refs, seq, first, last):
    q_ref, k_ref, v_ref, pq_ref, pk_ref = refs[:5]
    refs = refs[5:]
    if not first:
        acc_in_ref, st_in_ref = refs[:2]
        refs = refs[2:]
    if last:
        g_ref, o_ref = refs
    else:
        acc_out_ref, st_out_ref = refs

    start = pl.program_id(2) * Q_TILE
    if seq == K_WIN:
        win0 = 0
    else:
        win0 = pl.multiple_of(jnp.clip(start - HALF_SPAN, 0, seq - K_WIN), HALF_SPAN)
    q = q_ref[0]
    kw = k_ref[0, pl.ds(win0, K_WIN), :]
    vw = v_ref[0, pl.ds(win0, K_WIN), :]

    dist = jnp.abs(pq_ref[0] - pk_ref[0]).astype(F32)
    qi = start + lax.broadcasted_iota(jnp.int32, (Q_TILE, 1), 0)
    ki = win0 + lax.broadcasted_iota(jnp.int32, (1, K_WIN), 1)
    dist = jnp.where(jnp.abs(ki - qi) <= HALF_SPAN, dist, MASK_DIST)

    lane = lax.broadcasted_iota(jnp.int32, (Q_TILE, LANES), 1)
    stats = jnp.zeros((Q_TILE, LANES), F32)
    outs = []
    for hd in range(A_HEADS):
        c = slice(hd * A_HEAD_DIM, (hd + 1) * A_HEAD_DIM)
        slope = 2.0 ** -(hd + 1) * LOG2E
        s = _dot_nt(q[:, c], kw[:, c]) - slope * dist
        m = jnp.max(s, axis=-1, keepdims=True)
        if not first:
            m_in = st_in_ref[0, :, hd:hd + 1]
            l_in = st_in_ref[0, :, A_HEADS + hd:A_HEADS + hd + 1]
            m = jnp.maximum(m, m_in)
        p = jnp.exp2(s - m)
        l = jnp.sum(p, axis=-1, keepdims=True)
        acc = _dot(p.astype(BF16), vw[:, c])
        if not first:
            a = jnp.exp2(m_in - m)
            l = l + a * l_in
            acc = acc + a * acc_in_ref[0, :, c]
        if last:
            outs.append(acc / l)
        else:
            outs.append(acc)
            stats = jnp.where(lane == hd, m, stats)
            stats = jnp.where(lane == A_HEADS + hd, l, stats)
    out = jnp.concatenate(outs, axis=1)
    if last:
        o_ref[0] = _rms(out, g_ref[...]).astype(BF16)
    else:
        acc_out_ref[0] = out
        st_out_ref[0] = stats


def _dilated_branch(qa, ka, va, positions, dil, carry, gout):
    B, S, W = qa.shape
    seq = S // dil
    nblk = seq // Q_TILE
    first, last = carry is None, gout is not None

    view = lambda a: a.reshape(B, seq, dil * a.shape[-1])
    pos_sub = positions.reshape(B, seq, dil).transpose(0, 2, 1).reshape(B * dil, seq)
    pos_col = pos_sub[:, :, None]
    starts = [min(max(i * Q_TILE - HALF_SPAN, 0), seq - K_WIN) for i in range(nblk)]
    pos_win = jnp.stack([pos_sub[:, s:s + K_WIN] for s in starts], axis=1).reshape(B * dil * nblk, 1, K_WIN)

    tile = lambda n: pl.BlockSpec((1, Q_TILE, n), lambda b, r, i: (b, i, r))
    whole = pl.BlockSpec((1, seq, W), lambda b, r, i: (b, 0, r))
    in_specs = [tile(W), whole, whole,
                pl.BlockSpec((1, Q_TILE, 1), lambda b, r, i: (b * dil + r, i, 0)),
                pl.BlockSpec((1, 1, K_WIN), lambda b, r, i: ((b * dil + r) * nblk + i, 0, 0))]
    args = [view(qa), view(ka), view(va), pos_col, pos_win]
    if not first:
        in_specs += [tile(W), tile(LANES)]
        args += [view(carry[0]), view(carry[1])]
    if last:
        in_specs += [pl.BlockSpec((1, W), lambda b, r, i: (0, 0))]
        args += [gout]
        out_specs = tile(W)
        out_shape = jax.ShapeDtypeStruct((B, seq, dil * W), BF16)
    else:
        out_specs = [tile(W), tile(LANES)]
        out_shape = [jax.ShapeDtypeStruct((B, seq, dil * W), F32),
                     jax.ShapeDtypeStruct((B, seq, dil * LANES), F32)]
    res = pl.pallas_call(
        functools.partial(_dilated_kernel, seq=seq, first=first, last=last),
        grid=(B, dil, nblk),
        in_specs=in_specs,
        out_specs=out_specs,
        out_shape=out_shape,
        compiler_params=_params("arbitrary", "arbitrary", "arbitrary"),
        name=f"dilated_{dil}",
    )(*args)
    if last:
        return res.reshape(B, S, W)
    return res[0].reshape(B, S, W), res[1].reshape(B, S, LANES)


def _mla_kernel(q_ref, k_ref, v_ref, o_ref, *, tk):
    q = q_ref[0, 0]
    tq = q.shape[0]
    seq = k_ref.shape[2]

    def body(c, carry):
        m, l, acc = carry
        off = pl.multiple_of(c * tk, tk)
        s = _dot_nt(q, k_ref[0, 0, pl.ds(off, tk), :])
        m_new = jnp.maximum(m, jnp.max(s, axis=-1, keepdims=True))
        p = jnp.exp2(s - m_new)
        a = jnp.exp2(m - m_new)
        l = a * l + jnp.sum(p, axis=-1, keepdims=True)
        acc = a * acc + _dot(p.astype(BF16), v_ref[0, 0, pl.ds(off, tk), :])
        return m_new, l, acc

    init = (jnp.full((tq, 1), -jnp.inf, F32), jnp.zeros((tq, 1), F32), jnp.zeros((tq, B_V), F32))
    _, l, acc = lax.fori_loop(0, seq // tk, body, init)
    o_ref[0] = (acc / l).astype(BF16)


def _mla_attention(qb, kb, vb, tq, tk):
    B, H, S, _ = qb.shape
    return pl.pallas_call(
        functools.partial(_mla_kernel, tk=tk),
        grid=(B, H, S // tq),
        in_specs=[
            pl.BlockSpec((1, 1, tq, B_QK), lambda b, h, i: (b, h, i, 0)),
            pl.BlockSpec((1, 1, S, B_QK), lambda b, h, i: (b, h, 0, 0)),
            pl.BlockSpec((1, 1, S, B_V), lambda b, h, i: (b, h, 0, 0)),
        ],
        out_specs=pl.BlockSpec((1, tq, B_V), lambda b, h, i: (b, i, h)),
        out_shape=jax.ShapeDtypeStruct((B, S, H * B_V), BF16),
        compiler_params=_params("arbitrary", "arbitrary", "arbitrary"),
        name="mla_attn",
    )(qb, kb, vb)


def _mix_out_kernel(x_ref, oa_ref, ob_ref, mkv_ref, gb_ref, wout_ref, nq_ref, wmq_ref, wmo_ref, o_ref):
    ob = _rms(ob_ref[0], gb_ref[...]).astype(BF16)
    x1 = x_ref[0] + _dot(oa_ref[0], wout_ref[0:A_WIDTH, :]) + _dot(ob, wout_ref[A_WIDTH:, :])

    mq = (_dot(_rms(x1, nq_ref[...]).astype(BF16), wmq_ref[...]) * _MQ_SCALE).astype(BF16)
    outs = []
    for hd in range(M_HEADS):
        c = slice(hd * M_HEAD_DIM, (hd + 1) * M_HEAD_DIM)
        s = _dot_nt(mq[:, c], mkv_ref[0, :, c])
        p = jnp.exp2(s - jnp.max(s, axis=-1, keepdims=True))
        l = jnp.sum(p, axis=-1, keepdims=True)
        cv = slice(M_WIDTH + hd * M_HEAD_DIM, M_WIDTH + (hd + 1) * M_HEAD_DIM)
        outs.append(_dot(p.astype(BF16), mkv_ref[0, :, cv]) / l)
    mo = jnp.concatenate(outs, axis=1).astype(BF16)
    o_ref[0] = x1 + _dot(mo, wmo_ref[...])


def _mix_out(x, oa, ob, mkv, gb, wout, nq, wmq, wmo, tm):
    B, S, D = x.shape
    const = lambda shape: pl.BlockSpec(shape, lambda b, i: (0,) * len(shape))
    tok = lambda n: pl.BlockSpec((1, tm, n), lambda b, i: (b, i, 0))
    return pl.pallas_call(
        _mix_out_kernel,
        grid=(B, S // tm),
        in_specs=[tok(D), tok(A_WIDTH), tok(B_WIDTH),
                  pl.BlockSpec((1, N_MEM, 2 * M_WIDTH), lambda b, i: (b, 0, 0)),
                  const((1, B_WIDTH)), const(wout.shape), const((1, D)), const(wmq.shape), const(wmo.shape)],
        out_specs=tok(D),
        out_shape=jax.ShapeDtypeStruct((B, S, D), F32),
        compiler_params=_params("arbitrary", "arbitrary"),
        name="mix_out",
    )(x, oa, ob, mkv, gb, wout, nq, wmq, wmo)


def _ffn_kernel(x_ref, nf_ref, wg_ref, wu_ref, wd_ref, nfin_ref, o_ref):
    x = x_ref[...]
    hf = _rms(x, nf_ref[...]).astype(BF16)
    g = _dot(hf, wg_ref[...])
    u = _dot(hf, wu_ref[...])
    a = (g / (1.0 + jnp.exp(-g)) * u).astype(BF16)
    y = x + _dot(a, wd_ref[...])
    o_ref[...] = _rms(y, nfin_ref[...])


def _ffn(x, nf, wg, wu, wd, nfin, tm):
    T, D = x.shape
    const = lambda shape: pl.BlockSpec(shape, lambda i: (0,) * len(shape), pipeline_mode=pl.Buffered(1))
    tok = pl.BlockSpec((tm, D), lambda i: (i, 0))
    return pl.pallas_call(
        _ffn_kernel,
        grid=(T // tm,),
        in_specs=[tok, const((1, D)), const(wg.shape), const(wu.shape), const(wd.shape), const((1, D))],
        out_specs=tok,
        out_shape=jax.ShapeDtypeStruct((T, D), F32),
        compiler_params=_params("arbitrary"),
        name="ffn",
    )(x, nf, wg, wu, wd, nfin)


def _rotate_half_cols(w):
    half = w.shape[-1] // 2
    return jnp.concatenate([-w[..., half:], w[..., :half]], axis=-1)


def kernel(x, mem, positions, norm_mix, w_in, q_norm, w_q_up, kv_norm, w_kv_up, gout_a, gout_b, w_out,
           norm_mem_q, norm_mem_kv, w_mq, w_mkv, w_mo, norm_ffn, w_gate, w_up, w_down, norm_final):
    B, S, D = x.shape
    depth = w_in.shape[0]
    half = B_ROPE // 2
    inv_freq = ROPE_THETA ** (-jnp.arange(half, dtype=F32) / half)
    invf = jnp.tile(inv_freq, LANES // half)[None, :]
    pos_col = positions[:, :, None]

    for l in range(depth):
        w_kr = w_in[l][:, _C_KR:]
        win = jnp.concatenate([w_in[l], _rotate_half_cols(w_kr)], axis=1).astype(BF16)
        wq3 = w_q_up[l].reshape(Q_LORA, B_HEADS, B_QK)
        wq_pe = wq3[:, :, B_NOPE:]
        wq = jnp.concatenate([wq3[:, :, :B_NOPE].reshape(Q_LORA, -1), wq_pe.reshape(Q_LORA, -1),
                              _rotate_half_cols(wq_pe).reshape(Q_LORA, -1)], axis=1).astype(BF16)

        qa, ka, va, qb, kb, vb = _in_proj(x, pos_col, invf, norm_mix[l][None], win, q_norm[l][None], wq,
                                          kv_norm[l][None], w_kv_up[l].astype(BF16), tm=512)

        carry = None
        for dil in DILATIONS[:-1]:
            carry = _dilated_branch(qa, ka, va, positions, dil, carry, None)
        oa = _dilated_branch(qa, ka, va, positions, DILATIONS[-1], carry, gout_a[l][None])

        ob = _mla_attention(qb, kb, vb, tq=256, tk=512)

        mkv = _mem_kv(mem, norm_mem_kv[l][None], w_mkv[l].astype(BF16))
        x = _mix_out(x, oa, ob, mkv, gout_b[l][None], w_out[l].astype(BF16), norm_mem_q[l][None],
                     w_mq[l].astype(BF16), w_mo[l].astype(BF16), tm=512)

        last_norm = norm_final[None] if l == depth - 1 else None
        assert last_norm is not None, "the ffn call applies the final norm; depth is 1 for this problem"
        x = _ffn(x.reshape(B * S, D), norm_ffn[l][None], w_gate[l].astype(BF16), w_up[l].astype(BF16),
                 w_down[l].astype(BF16), last_norm, tm=512).reshape(B, S, D)
    return x
```

```python
import functools
import math

import jax
import jax.numpy as jnp
from jax import lax
from jax.experimental import pallas as pl
from jax.experimental.pallas import tpu as pltpu

F32 = jnp.float32
BF16 = jnp.bfloat16

EPS = 1e-6
D_MODEL = 1024
N_MEM = 256

A_HEADS = 8
A_HEAD_DIM = 64
A_WIDTH = A_HEADS * A_HEAD_DIM
DILATIONS = (1, 4, 16)
HALF_SPAN = 64
Q_TILE = 128
K_WIN = Q_TILE + 2 * HALF_SPAN

B_HEADS = 4
B_NOPE = 128
B_ROPE = 64
B_QK = B_NOPE + B_ROPE
B_V = 128
B_WIDTH = B_HEADS * B_V
Q_LORA = 384
KV_LORA = 256
ROPE_THETA = 10000.0

M_HEADS = 4
M_HEAD_DIM = 128
M_WIDTH = M_HEADS * M_HEAD_DIM

LOG2E = math.log2(math.e)
MASK_DIST = 1e30
LANES = 128
VMEM_LIMIT = 56 * 1024 * 1024


def _rms(x, g):
    x = x.astype(F32)
    return x * lax.rsqrt(jnp.mean(x * x, axis=-1, keepdims=True) + EPS) * g


def _dot(a, b):
    return jnp.dot(a, b, preferred_element_type=F32)


def _dot_nt(a, b):
    return lax.dot_general(a, b, (((1,), (1,)), ((), ())), preferred_element_type=F32)


def _params(*sem):
    return pltpu.CompilerParams(dimension_semantics=sem, vmem_limit_bytes=VMEM_LIMIT)


def _mem_kv_kernel(mem_ref, g_ref, w_ref, o_ref):
    o_ref[0] = _dot(_rms(mem_ref[0], g_ref[...]).astype(BF16), w_ref[...]).astype(BF16)


def _mem_kv(mem, g, w):
    B, M, D = mem.shape
    N = w.shape[1]
    return pl.pallas_call(
        _mem_kv_kernel,
        grid=(B,),
        in_specs=[
            pl.BlockSpec((1, M, D), lambda b: (b, 0, 0)),
            pl.BlockSpec((1, D), lambda b: (0, 0)),
            pl.BlockSpec((D, N), lambda b: (0, 0)),
        ],
        out_specs=pl.BlockSpec((1, M, N), lambda b: (b, 0, 0)),
        out_shape=jax.ShapeDtypeStruct((B, M, N), BF16),
        compiler_params=_params("arbitrary"),
        name="mem_kv",
    )(mem, g, w)


_C_QA, _C_KA, _C_VA, _C_CQ, _C_CKV, _C_KR, _C_END = 0, 512, 1024, 1536, 1920, 2176, 2304
_QA_SCALE = A_HEAD_DIM ** -0.5 * LOG2E
_QB_SCALE = B_QK ** -0.5 * LOG2E
_MQ_SCALE = M_HEAD_DIM ** -0.5 * LOG2E


def _in_proj_kernel(x_ref, pos_ref, invf_ref, nmix_ref, win_ref, qn_ref, wq_ref, kvn_ref, wkv_ref,
                    qa_ref, ka_ref, va_ref, qb_ref, kb_ref, vb_ref):
    h = _rms(x_ref[0], nmix_ref[...]).astype(BF16)

    def seg(a, b):
        return _dot(h, win_ref[:, a:b])

    qa_ref[0] = (seg(_C_QA, _C_KA) * _QA_SCALE).astype(BF16)
    ka_ref[0] = seg(_C_KA, _C_VA).astype(BF16)
    va_ref[0] = seg(_C_VA, _C_CQ).astype(BF16)
    cq = seg(_C_CQ, _C_CKV)
    ckv = seg(_C_CKV, _C_KR)
    kr2 = seg(_C_KR, _C_END)

    ang = pos_ref[0].astype(F32) * invf_ref[...]
    cos4 = jnp.cos(ang)
    sin4 = jnp.sin(ang)
    cos8 = jnp.concatenate([cos4, cos4], axis=1)
    sin8 = jnp.concatenate([sin4, sin4], axis=1)

    qb = _dot(_rms(cq, qn_ref[...]).astype(BF16), wq_ref[...])
    n0 = B_HEADS * B_NOPE
    n1 = n0 + B_HEADS * B_ROPE
    q_pe = (qb[:, n0:n1] * cos8 + qb[:, n1:] * sin8) * _QB_SCALE

    lane = lax.broadcasted_iota(jnp.int32, kr2.shape, 1)
    t = kr2 * jnp.where(lane < B_ROPE, cos4, sin4)
    k_pe = (t[:, :B_ROPE] + t[:, B_ROPE:]).astype(BF16)

    kvb = _dot(_rms(ckv, kvn_ref[...]).astype(BF16), wkv_ref[...])
    for hd in range(B_HEADS):
        qb_ref[0, hd, :, 0:B_NOPE] = (qb[:, hd * B_NOPE:(hd + 1) * B_NOPE] * _QB_SCALE).astype(BF16)
        qb_ref[0, hd, :, B_NOPE:B_QK] = q_pe[:, hd * B_ROPE:(hd + 1) * B_ROPE].astype(BF16)
        c0 = hd * (B_NOPE + B_V)
        kb_ref[0, hd, :, 0:B_NOPE] = kvb[:, c0:c0 + B_NOPE].astype(BF16)
        kb_ref[0, hd, :, B_NOPE:B_QK] = k_pe
        vb_ref[0, hd] = kvb[:, c0 + B_NOPE:c0 + B_NOPE + B_V].astype(BF16)


def _in_proj(x, pos_col, invf, nmix, win, qn, wq, kvn, wkv, tm):
    B, S, D = x.shape
    const = lambda shape: pl.BlockSpec(shape, lambda b, i: (0,) * len(shape))
    tok = lambda n: pl.BlockSpec((1, tm, n), lambda b, i: (b, i, 0))
    head = lambda n: pl.BlockSpec((1, B_HEADS, tm, n), lambda b, i: (b, 0, i, 0))
    return pl.pallas_call(
        _in_proj_kernel,
        grid=(B, S // tm),
        in_specs=[tok(D), tok(1), const((1, LANES)), const((1, D)), const(win.shape),
                  const((1, Q_LORA)), const(wq.shape), const((1, KV_LORA)), const(wkv.shape)],
        out_specs=[tok(A_WIDTH), tok(A_WIDTH), tok(A_WIDTH), head(B_QK), head(B_QK), head(B_V)],
        out_shape=[jax.ShapeDtypeStruct((B, S, A_WIDTH), BF16)] * 3
        + [jax.ShapeDtypeStruct((B, B_HEADS, S, B_QK), BF16)] * 2
        + [jax.ShapeDtypeStruct((B, B_HEADS, S, B_V), BF16)],
        compiler_params=_params("arbitrary", "arbitrary"),
        name="in_proj",
    )(x, pos_col, invf, nmix, win, qn, wq, kvn, wkv)


def _dilated_kernel(*refs, seq, first, last):
    q_ref, k_ref, v_ref, pq_ref, pk_ref = refs[:5]
    refs = refs[5:]
    if not first:
        acc_in_ref, st_in_ref = refs[:2]
        refs = refs[2:]
    if last:
        g_ref, o_ref = refs
    else:
        acc_out_ref, st_out_ref = refs

    start = pl.program_id(2) * Q_TILE
    if seq == K_WIN:
        win0 = 0
    else:
        win0 = pl.multiple_of(jnp.clip(start - HALF_SPAN, 0, seq - K_WIN), HALF_SPAN)
    q = q_ref[0]
    kw = k_ref[0, pl.ds(win0, K_WIN), :]
    vw = v_ref[0, pl.ds(win0, K_WIN), :]

    dist = jnp.abs(pq_ref[0] - pk_ref[0]).astype(F32)
    qi = start + lax.broadcasted_iota(jnp.int32, (Q_TILE, 1), 0)
    ki = win0 + lax.broadcasted_iota(jnp.int32, (1, K_WIN), 1)
    dist = jnp.where(jnp.abs(ki - qi) <= HALF_SPAN, dist, MASK_DIST)

    lane = lax.broadcasted_iota(jnp.int32, (Q_TILE, LANES), 1)
    stats = jnp.zeros((Q_TILE, LANES), F32)
    outs = []
    for hd in range(A_HEADS):
        c = slice(hd * A_HEAD_DIM, (hd + 1) * A_HEAD_DIM)
        slope = 2.0 ** -(hd + 1) * LOG2E
        s = _dot_nt(q[:, c], kw[:, c]) - slope * dist
        m = jnp.max(s, axis=-1, keepdims=True)
        if not first:
            m_in = st_in_ref[0, :, hd:hd + 1]
            l_in = st_in_ref[0, :, A_HEADS + hd:A_HEADS + hd + 1]
            m = jnp.maximum(m, m_in)
        p = jnp.exp2(s - m)
        l = jnp.sum(p, axis=-1, keepdims=True)
        acc = _dot(p.astype(BF16), vw[:, c])
        if not first:
            a = jnp.exp2(m_in - m)
            l = l + a * l_in
            acc = acc + a * acc_in_ref[0, :, c]
        if last:
            outs.append(acc / l)
        else:
            outs.append(acc)
            stats = jnp.where(lane == hd, m, stats)
            stats = jnp.where(lane == A_HEADS + hd, l, stats)
    out = jnp.concatenate(outs, axis=1)
    if last:
        o_ref[0] = _rms(out, g_ref[...]).astype(BF16)
    else:
        acc_out_ref[0] = out
        st_out_ref[0] = stats


def _dilated_branch(qa, ka, va, positions, dil, carry, gout):
    B, S, W = qa.shape
    seq = S // dil
    nblk = seq // Q_TILE
    first, last = carry is None, gout is not None

    view = lambda a: a.reshape(B, seq, dil * a.shape[-1])
    pos_sub = positions.reshape(B, seq, dil).transpose(0, 2, 1).reshape(B * dil, seq)
    pos_col = pos_sub[:, :, None]
    starts = [min(max(i * Q_TILE - HALF_SPAN, 0), seq - K_WIN) for i in range(nblk)]
    pos_win = jnp.stack([pos_sub[:, s:s + K_WIN] for s in starts], axis=1).reshape(B * dil * nblk, 1, K_WIN)

    tile = lambda n: pl.BlockSpec((1, Q_TILE, n), lambda b, r, i: (b, i, r))
    whole = pl.BlockSpec((1, seq, W), lambda b, r, i: (b, 0, r))
    in_specs = [tile(W), whole, whole,
                pl.BlockSpec((1, Q_TILE, 1), lambda b, r, i: (b * dil + r, i, 0)),
                pl.BlockSpec((1, 1, K_WIN), lambda b, r, i: ((b * dil + r) * nblk + i, 0, 0))]
    args = [view(qa), view(ka), view(va), pos_col, pos_win]
    if not first:
        in_specs += [tile(W), tile(LANES)]
        args += [view(carry[0]), view(carry[1])]
    if last:
        in_specs += [pl.BlockSpec((1, W), lambda b, r, i: (0, 0))]
        args += [gout]
        out_specs = tile(W)
        out_shape = jax.ShapeDtypeStruct((B, seq, dil * W), BF16)
    else:
        out_specs = [tile(W), tile(LANES)]
        out_shape = [jax.ShapeDtypeStruct((B, seq, dil * W), F32),
                     jax.ShapeDtypeStruct((B, seq, dil * LANES), F32)]
    res = pl.pallas_call(
        functools.partial(_dilated_kernel, seq=seq, first=first, last=last),
        grid=(B, dil, nblk),
        in_specs=in_specs,
        out_specs=out_specs,
        out_shape=out_shape,
        compiler_params=_params("arbitrary", "arbitrary", "arbitrary"),
        name=f"dilated_{dil}",
    )(*args)
    if last:
        return res.reshape(B, S, W)
    return res[0].reshape(B, S, W), res[1].reshape(B, S, LANES)


def _mla_kernel(q_ref, k_ref, v_ref, o_ref, *, tk, n_sub, unroll):
    tq = q_ref.shape[2]
    sub = tq // n_sub
    seq = k_ref.shape[2]
    ones = jnp.ones((tk, B_V), BF16)

    def body(c, carry):
        off = pl.multiple_of(c * tk, tk)
        kc = k_ref[0, 0, pl.ds(off, tk), :]
        vc = jnp.concatenate([v_ref[0, 0, pl.ds(off, tk), :], ones], axis=1)
        new = []
        for j in range(n_sub):
            m, acc = carry[j]
            s = _dot_nt(q_ref[0, 0, j * sub:(j + 1) * sub, :], kc)
            m_new = jnp.maximum(m, jnp.max(s, axis=-1, keepdims=True))
            p = jnp.exp2(s - m_new).astype(BF16)
            acc = jnp.exp2(m - m_new) * acc + _dot(p, vc)
            new.append((m_new, acc))
        return tuple(new)

    init = tuple((jnp.full((sub, 1), -jnp.inf, F32), jnp.zeros((sub, 2 * B_V), F32)) for _ in range(n_sub))
    res = lax.fori_loop(0, seq // tk, body, init, unroll=unroll)
    for j in range(n_sub):
        acc = res[j][1]
        o_ref[0, j * sub:(j + 1) * sub, :] = (acc[:, :B_V] / acc[:, B_V:]).astype(BF16)


def _mla_attention(qb, kb, vb, tq, tk, n_sub, unroll):
    B, H, S, _ = qb.shape
    return pl.pallas_call(
        functools.partial(_mla_kernel, tk=tk, n_sub=n_sub, unroll=unroll),
        grid=(B, H, S // tq),
        in_specs=[
            pl.BlockSpec((1, 1, tq, B_QK), lambda b, h, i: (b, h, i, 0)),
            pl.BlockSpec((1, 1, S, B_QK), lambda b, h, i: (b, h, 0, 0)),
            pl.BlockSpec((1, 1, S, B_V), lambda b, h, i: (b, h, 0, 0)),
        ],
        out_specs=pl.BlockSpec((1, tq, B_V), lambda b, h, i: (b, i, h)),
        out_shape=jax.ShapeDtypeStruct((B, S, H * B_V), BF16),
        compiler_params=_params("arbitrary", "arbitrary", "arbitrary"),
        name="mla_attn",
    )(qb, kb, vb)


def _mix_out_kernel(x_ref, oa_ref, ob_ref, mkv_ref, gb_ref, wout_ref, nq_ref, wmq_ref, wmo_ref, o_ref):
    ob = _rms(ob_ref[0], gb_ref[...]).astype(BF16)
    x1 = x_ref[0] + _dot(oa_ref[0], wout_ref[0:A_WIDTH, :]) + _dot(ob, wout_ref[A_WIDTH:, :])

    mq = (_dot(_rms(x1, nq_ref[...]).astype(BF16), wmq_ref[...]) * _MQ_SCALE).astype(BF16)
    outs = []
    for hd in range(M_HEADS):
        c = slice(hd * M_HEAD_DIM, (hd + 1) * M_HEAD_DIM)
        s = _dot_nt(mq[:, c], mkv_ref[0, :, c])
        p = jnp.exp2(s - jnp.max(s, axis=-1, keepdims=True))
        l = jnp.sum(p, axis=-1, keepdims=True)
        cv = slice(M_WIDTH + hd * M_HEAD_DIM, M_WIDTH + (hd + 1) * M_HEAD_DIM)
        outs.append(_dot(p.astype(BF16), mkv_ref[0, :, cv]) / l)
    mo = jnp.concatenate(outs, axis=1).astype(BF16)
    o_ref[0] = x1 + _dot(mo, wmo_ref[...])


def _mix_out(x, oa, ob, mkv, gb, wout, nq, wmq, wmo, tm):
    B, S, D = x.shape
    const = lambda shape: pl.BlockSpec(shape, lambda b, i: (0,) * len(shape))
    tok = lambda n: pl.BlockSpec((1, tm, n), lambda b, i: (b, i, 0))
    return pl.pallas_call(
        _mix_out_kernel,
        grid=(B, S // tm),
        in_specs=[tok(D), tok(A_WIDTH), tok(B_WIDTH),
                  pl.BlockSpec((1, N_MEM, 2 * M_WIDTH), lambda b, i: (b, 0, 0)),
                  const((1, B_WIDTH)), const(wout.shape), const((1, D)), const(wmq.shape), const(wmo.shape)],
        out_specs=tok(D),
        out_shape=jax.ShapeDtypeStruct((B, S, D), F32),
        compiler_params=_params("arbitrary", "arbitrary"),
        name="mix_out",
    )(x, oa, ob, mkv, gb, wout, nq, wmq, wmo)


def _ffn_kernel(x_ref, nf_ref, wg_ref, wu_ref, wd_ref, nfin_ref, o_ref):
    x = x_ref[...]
    hf = _rms(x, nf_ref[...]).astype(BF16)
    g = _dot(hf, wg_ref[...])
    u = _dot(hf, wu_ref[...])
    a = (g / (1.0 + jnp.exp(-g)) * u).astype(BF16)
    y = x + _dot(a, wd_ref[...])
    o_ref[...] = _rms(y, nfin_ref[...])


def _ffn(x, nf, wg, wu, wd, nfin, tm):
    T, D = x.shape
    const = lambda shape: pl.BlockSpec(shape, lambda i: (0,) * len(shape), pipeline_mode=pl.Buffered(1))
    tok = pl.BlockSpec((tm, D), lambda i: (i, 0))
    return pl.pallas_call(
        _ffn_kernel,
        grid=(T // tm,),
        in_specs=[tok, const((1, D)), const(wg.shape), const(wu.shape), const(wd.shape), const((1, D))],
        out_specs=tok,
        out_shape=jax.ShapeDtypeStruct((T, D), F32),
        compiler_params=_params("arbitrary"),
        name="ffn",
    )(x, nf, wg, wu, wd, nfin)


def _rotate_half_cols(w):
    half = w.shape[-1] // 2
    return jnp.concatenate([-w[..., half:], w[..., :half]], axis=-1)


def kernel(x, mem, positions, norm_mix, w_in, q_norm, w_q_up, kv_norm, w_kv_up, gout_a, gout_b, w_out,
           norm_mem_q, norm_mem_kv, w_mq, w_mkv, w_mo, norm_ffn, w_gate, w_up, w_down, norm_final):
    B, S, D = x.shape
    depth = w_in.shape[0]
    half = B_ROPE // 2
    inv_freq = ROPE_THETA ** (-jnp.arange(half, dtype=F32) / half)
    invf = jnp.tile(inv_freq, LANES // half)[None, :]
    pos_col = positions[:, :, None]

    for l in range(depth):
        w_kr = w_in[l][:, _C_KR:]
        win = jnp.concatenate([w_in[l], _rotate_half_cols(w_kr)], axis=1).astype(BF16)
        wq3 = w_q_up[l].reshape(Q_LORA, B_HEADS, B_QK)
        wq_pe = wq3[:, :, B_NOPE:]
        wq = jnp.concatenate([wq3[:, :, :B_NOPE].reshape(Q_LORA, -1), wq_pe.reshape(Q_LORA, -1),
                              _rotate_half_cols(wq_pe).reshape(Q_LORA, -1)], axis=1).astype(BF16)

        qa, ka, va, qb, kb, vb = _in_proj(x, pos_col, invf, norm_mix[l][None], win, q_norm[l][None], wq,
                                          kv_norm[l][None], w_kv_up[l].astype(BF16), tm=512)

        carry = None
        for dil in DILATIONS[:-1]:
            carry = _dilated_branch(qa, ka, va, positions, dil, carry, None)
        oa = _dilated_branch(qa, ka, va, positions, DILATIONS[-1], carry, gout_a[l][None])

        ob = _mla_attention(qb, kb, vb, tq=1024, tk=512, n_sub=4, unroll=True)

        mkv = _mem_kv(mem, norm_mem_kv[l][None], w_mkv[l].astype(BF16))
        x = _mix_out(x, oa, ob, mkv, gout_b[l][None], w_out[l].astype(BF16), norm_mem_q[l][None],
                     w_mq[l].astype(BF16), w_mo[l].astype(BF16), tm=512)

        last_norm = norm_final[None] if l == depth - 1 else None
        assert last_norm is not None, "the ffn call applies the final norm; depth is 1 for this problem"
        x = _ffn(x.reshape(B * S, D), norm_ffn[l][None], w_gate[l].astype(BF16), w_up[l].astype(BF16),
                 w_down[l].astype(BF16), last_norm, tm=512).reshape(B, S, D)
    return x
```

```python
import functools
import math

import jax
import jax.numpy as jnp
import numpy as np
from jax import lax
from jax.experimental import pallas as pl
from jax.experimental.pallas import tpu as pltpu

F32 = jnp.float32
BF16 = jnp.bfloat16

EPS = 1e-6
D_MODEL = 1024
N_MEM = 256

A_HEADS = 8
A_HEAD_DIM = 64
A_WIDTH = A_HEADS * A_HEAD_DIM
DILATIONS = (1, 4, 16)
HALF_SPAN = 64
Q_TILE = 128
K_WIN = Q_TILE + 2 * HALF_SPAN
PERM = 256
D1_TILES = 4

B_HEADS = 4
B_NOPE = 128
B_ROPE = 64
B_QK = B_NOPE + B_ROPE
B_V = 128
B_WIDTH = B_HEADS * B_V
Q_LORA = 384
KV_LORA = 256
ROPE_THETA = 10000.0

M_HEADS = 4
M_HEAD_DIM = 128
M_WIDTH = M_HEADS * M_HEAD_DIM

LOG2E = math.log2(math.e)
MASK_DIST = 1e30
LANES = 128
VMEM_LIMIT = 56 * 1024 * 1024


def _rms(x, g):
    x = x.astype(F32)
    return x * lax.rsqrt(jnp.mean(x * x, axis=-1, keepdims=True) + EPS) * g


def _dot(a, b):
    return jnp.dot(a, b, preferred_element_type=F32)


def _dot_nt(a, b):
    return lax.dot_general(a, b, (((1,), (1,)), ((), ())), preferred_element_type=F32)


def _split_bf16(x):
    hi = x.astype(BF16)
    return hi, (x - hi.astype(F32)).astype(BF16)


def _params(*sem):
    return pltpu.CompilerParams(dimension_semantics=sem, vmem_limit_bytes=VMEM_LIMIT)


def _class_perm(dil):
    idx = np.arange(PERM)
    r, a = idx // (PERM // dil), idx % (PERM // dil)
    p = np.zeros((PERM, PERM), np.float32)
    p[idx, dil * a + r] = 1.0
    return p


def _mem_kv_kernel(mem_ref, g_ref, w_ref, o_ref):
    o_ref[0] = _dot(_rms(mem_ref[0], g_ref[...]).astype(BF16), w_ref[...]).astype(BF16)


def _mem_kv(mem, g, w):
    B, M, D = mem.shape
    N = w.shape[1]
    return pl.pallas_call(
        _mem_kv_kernel,
        grid=(B,),
        in_specs=[
            pl.BlockSpec((1, M, D), lambda b: (b, 0, 0)),
            pl.BlockSpec((1, D), lambda b: (0, 0)),
            pl.BlockSpec((D, N), lambda b: (0, 0)),
        ],
        out_specs=pl.BlockSpec((1, M, N), lambda b: (b, 0, 0)),
        out_shape=jax.ShapeDtypeStruct((B, M, N), BF16),
        compiler_params=_params("arbitrary"),
        name="mem_kv",
    )(mem, g, w)


_C_QA, _C_KA, _C_VA, _C_CQ, _C_CKV, _C_KR, _C_END = 0, 512, 1024, 1536, 1920, 2176, 2304
_QA_SCALE = A_HEAD_DIM ** -0.5 * LOG2E
_QB_SCALE = B_QK ** -0.5 * LOG2E
_MQ_SCALE = M_HEAD_DIM ** -0.5 * LOG2E
_REGROUPED = DILATIONS[1:]


def _in_proj_kernel(x_ref, pos_ref, invf_ref, nmix_ref, win_ref, qn_ref, wq_ref, kvn_ref, wkv_ref, *refs):
    perm_refs = refs[:len(_REGROUPED)]
    qa_ref, ka_ref, va_ref, qb_ref, kb_ref, vb_ref = refs[len(_REGROUPED):len(_REGROUPED) + 6]
    class_refs = refs[len(_REGROUPED) + 6:]
    tm = x_ref.shape[1]
    h = _rms(x_ref[0], nmix_ref[...]).astype(BF16)

    def seg(a, b):
        return _dot(h, win_ref[:, a:b])

    qkv = [(seg(_C_QA, _C_KA) * _QA_SCALE).astype(BF16), seg(_C_KA, _C_VA).astype(BF16),
           seg(_C_VA, _C_CQ).astype(BF16)]
    for val, ref in zip(qkv, (qa_ref, ka_ref, va_ref)):
        ref[0] = val
    for n, dil in enumerate(_REGROUPED):
        per = PERM // dil
        for part in range(tm // PERM):
            for val, ref in zip(qkv, class_refs[3 * n:3 * n + 3]):
                y = _dot(perm_refs[n][...], val[part * PERM:(part + 1) * PERM])
                ref[0, :, part * per:(part + 1) * per, :] = y.reshape(dil, per, A_WIDTH).astype(BF16)

    cq = seg(_C_CQ, _C_CKV)
    ckv = seg(_C_CKV, _C_KR)
    kr2 = seg(_C_KR, _C_END)

    ang = pos_ref[0].astype(F32) * invf_ref[...]
    cos4 = jnp.cos(ang)
    sin4 = jnp.sin(ang)
    cos8 = jnp.concatenate([cos4, cos4], axis=1)
    sin8 = jnp.concatenate([sin4, sin4], axis=1)

    qb = _dot(_rms(cq, qn_ref[...]).astype(BF16), wq_ref[...])
    n0 = B_HEADS * B_NOPE
    n1 = n0 + B_HEADS * B_ROPE
    q_pe = (qb[:, n0:n1] * cos8 + qb[:, n1:] * sin8) * _QB_SCALE

    lane = lax.broadcasted_iota(jnp.int32, kr2.shape, 1)
    t = kr2 * jnp.where(lane < B_ROPE, cos4, sin4)
    k_pe = (t[:, :B_ROPE] + t[:, B_ROPE:]).astype(BF16)

    kvb = _dot(_rms(ckv, kvn_ref[...]).astype(BF16), wkv_ref[...])
    for hd in range(B_HEADS):
        qb_ref[0, hd, :, 0:B_NOPE] = (qb[:, hd * B_NOPE:(hd + 1) * B_NOPE] * _QB_SCALE).astype(BF16)
        qb_ref[0, hd, :, B_NOPE:B_QK] = q_pe[:, hd * B_ROPE:(hd + 1) * B_ROPE].astype(BF16)
        c0 = hd * (B_NOPE + B_V)
        kb_ref[0, hd, :, 0:B_NOPE] = kvb[:, c0:c0 + B_NOPE].astype(BF16)
        kb_ref[0, hd, :, B_NOPE:B_QK] = k_pe
        vb_ref[0, hd] = kvb[:, c0 + B_NOPE:c0 + B_NOPE + B_V].astype(BF16)


def _in_proj(x, pos_col, invf, nmix, win, qn, wq, kvn, wkv, perms, tm):
    B, S, D = x.shape
    const = lambda shape: pl.BlockSpec(shape, lambda b, i: (0,) * len(shape))
    tok = lambda n: pl.BlockSpec((1, tm, n), lambda b, i: (b, i, 0))
    head = lambda n: pl.BlockSpec((1, B_HEADS, tm, n), lambda b, i: (b, 0, i, 0))
    cls = lambda dil: pl.BlockSpec((1, dil, tm // dil, A_WIDTH), lambda b, i: (b, 0, i, 0))
    out_specs = [tok(A_WIDTH)] * 3 + [head(B_QK), head(B_QK), head(B_V)]
    out_shape = ([jax.ShapeDtypeStruct((B, S, A_WIDTH), BF16)] * 3
                 + [jax.ShapeDtypeStruct((B, B_HEADS, S, B_QK), BF16)] * 2
                 + [jax.ShapeDtypeStruct((B, B_HEADS, S, B_V), BF16)])
    for dil in _REGROUPED:
        out_specs += [cls(dil)] * 3
        out_shape += [jax.ShapeDtypeStruct((B, dil, S // dil, A_WIDTH), BF16)] * 3
    res = pl.pallas_call(
        _in_proj_kernel,
        grid=(B, S // tm),
        in_specs=[tok(D), tok(1), const((1, LANES)), const((1, D)), const(win.shape),
                  const((1, Q_LORA)), const(wq.shape), const((1, KV_LORA)), const(wkv.shape)]
        + [const((PERM, PERM))] * len(perms),
        out_specs=out_specs,
        out_shape=out_shape,
        compiler_params=_params("arbitrary", "arbitrary"),
        name="in_proj",
    )(x, pos_col, invf, nmix, win, qn, wq, kvn, wkv, *perms)
    qkv = {1: tuple(a[:, None] for a in res[0:3])}
    for n, dil in enumerate(_REGROUPED):
        qkv[dil] = tuple(res[6 + 3 * n:9 + 3 * n])
    return qkv, res[3:6]


def _tile_attention(q, kw, vw, dist):
    lane = lax.broadcasted_iota(jnp.int32, (Q_TILE, LANES), 1)
    low = lane < A_HEAD_DIM
    ones = jnp.ones((K_WIN, LANES), BF16)
    m_tile = jnp.zeros((Q_TILE, LANES), F32)
    l_tile = jnp.ones((Q_TILE, LANES), F32)
    pairs = []
    for pair in range(A_HEADS // 2):
        c = slice(pair * LANES, (pair + 1) * LANES)
        qp, kp = q[:, c], kw[:, c]
        v_ext = jnp.concatenate([vw[:, c], ones], axis=1)
        acc = []
        for half in range(2):
            hd = 2 * pair + half
            qm = jnp.where(low if half == 0 else ~low, qp, jnp.zeros_like(qp))
            s = _dot_nt(qm, kp) - (2.0 ** -(hd + 1) * LOG2E) * dist
            m = jnp.max(s, axis=-1, keepdims=True)
            r = _dot(jnp.exp2(s - m).astype(BF16), v_ext)
            acc.append(r[:, :LANES])
            m_tile = jnp.where(lane == hd, m, m_tile)
            l_tile = jnp.where(lane == hd, r[:, LANES:], l_tile)
        pairs.append(jnp.where(low, acc[0], acc[1]))
    return jnp.concatenate(pairs, axis=1), jnp.concatenate([m_tile, l_tile], axis=1)


def _dilated_kernel(*refs, dil, seq):
    if dil == 1:
        q_ref, k_ref, v_ref, pq_ref, pk_ref, acc_ref, st_ref = refs
    else:
        q_ref, k_ref, v_ref, pq_ref, pk_ref, pt_ref, acc_ref, st_ref, acc_scr, st_scr = refs
    step = pl.program_id(1)

    def tile(cls, t_local, tile_idx):
        start = tile_idx * Q_TILE
        if seq == K_WIN:
            win0 = 0
        else:
            win0 = pl.multiple_of(jnp.clip(start - HALF_SPAN, 0, seq - K_WIN), HALF_SPAN)
        rows = pl.ds(t_local * Q_TILE, Q_TILE)
        dist = jnp.abs(pq_ref[0, cls, t_local] - pk_ref[0, t_local, cls]).astype(F32)
        qi = start + lax.broadcasted_iota(jnp.int32, (Q_TILE, 1), 0)
        ki = win0 + lax.broadcasted_iota(jnp.int32, (1, K_WIN), 1)
        dist = jnp.where(jnp.abs(ki - qi) <= HALF_SPAN, dist, MASK_DIST)
        return _tile_attention(q_ref[0, cls, rows, :], k_ref[0, cls, pl.ds(win0, K_WIN), :],
                               v_ref[0, cls, pl.ds(win0, K_WIN), :], dist)

    if dil == 1:
        for t in range(D1_TILES):
            acc, st = tile(0, t, step * D1_TILES + t)
            acc_ref[0, t * Q_TILE:(t + 1) * Q_TILE, :] = acc.astype(BF16)
            st_ref[0, t * Q_TILE:(t + 1) * Q_TILE, :] = st
        return

    def body(cls, carry):
        acc, st = tile(cls, 0, step)
        acc_scr[cls] = acc.astype(BF16)
        st_scr[cls] = st
        return carry

    if dil <= 4:
        for cls in range(dil):
            body(cls, 0)
    else:
        lax.fori_loop(0, dil, body, 0, unroll=2)

    per = PERM // dil
    for part in range(dil * Q_TILE // PERM):
        rows = slice(part * per, (part + 1) * per)
        out = slice(part * PERM, (part + 1) * PERM)
        acc = jnp.concatenate([acc_scr[cls, rows, :] for cls in range(dil)], axis=0)
        acc_ref[0, out, :] = _dot(pt_ref[...], acc).astype(BF16)
        hi, lo = _split_bf16(jnp.concatenate([st_scr[cls, rows, :] for cls in range(dil)], axis=0))
        st_ref[0, out, :] = _dot(pt_ref[...], hi) + _dot(pt_ref[...], lo)


def _dilated_branch(q, k, v, positions, dil, perm_t):
    B, _, seq, W = q.shape
    S = seq * dil
    nblk = seq // Q_TILE
    tps = D1_TILES if dil == 1 else 1
    pos_cls = positions.reshape(B, seq, dil).transpose(0, 2, 1)
    pos_q = pos_cls.reshape(B, dil, nblk, Q_TILE, 1)
    starts = [min(max(i * Q_TILE - HALF_SPAN, 0), seq - K_WIN) for i in range(nblk)]
    pos_k = jnp.stack([pos_cls[:, :, s:s + K_WIN] for s in starts], axis=1)[:, :, :, None, :]

    whole = pl.BlockSpec((1, dil, seq, W), lambda b, i: (b, 0, 0, 0))
    in_specs = [pl.BlockSpec((1, dil, tps * Q_TILE, W), lambda b, i: (b, 0, i, 0)), whole, whole,
                pl.BlockSpec((1, dil, tps, Q_TILE, 1), lambda b, i: (b, 0, i, 0, 0)),
                pl.BlockSpec((1, tps, dil, 1, K_WIN), lambda b, i: (b, i, 0, 0, 0))]
    args = [q, k, v, pos_q, pos_k]
    scratch = []
    if dil > 1:
        in_specs.append(pl.BlockSpec((PERM, PERM), lambda b, i: (0, 0)))
        args.append(perm_t)
        scratch = [pltpu.VMEM((dil, Q_TILE, W), BF16), pltpu.VMEM((dil, Q_TILE, 2 * LANES), F32)]
    chunk = dil * tps * Q_TILE
    return pl.pallas_call(
        functools.partial(_dilated_kernel, dil=dil, seq=seq),
        grid=(B, S // chunk),
        in_specs=in_specs,
        out_specs=[pl.BlockSpec((1, chunk, W), lambda b, i: (b, i, 0)),
                   pl.BlockSpec((1, chunk, 2 * LANES), lambda b, i: (b, i, 0))],
        out_shape=[jax.ShapeDtypeStruct((B, S, W), BF16), jax.ShapeDtypeStruct((B, S, 2 * LANES), F32)],
        scratch_shapes=scratch,
        compiler_params=_params("arbitrary", "arbitrary"),
        name=f"dilated_{dil}",
    )(*args)


def _mla_kernel(q_ref, k_ref, v_ref, o_ref, *, tk, n_sub):
    tq = q_ref.shape[2]
    sub = tq // n_sub
    seq = k_ref.shape[2]
    ones = jnp.ones((tk, B_V), BF16)

    def body(c, carry):
        off = pl.multiple_of(c * tk, tk)
        kc = k_ref[0, 0, pl.ds(off, tk), :]
        vc = jnp.concatenate([v_ref[0, 0, pl.ds(off, tk), :], ones], axis=1)
        new = []
        for j in range(n_sub):
            m, acc = carry[j]
            s = _dot_nt(q_ref[0, 0, j * sub:(j + 1) * sub, :], kc)
            m_new = jnp.maximum(m, jnp.max(s, axis=-1, keepdims=True))
            p = jnp.exp2(s - m_new).astype(BF16)
            acc = jnp.exp2(m - m_new) * acc + _dot(p, vc)
            new.append((m_new, acc))
        return tuple(new)

    init = tuple((jnp.full((sub, 1), -jnp.inf, F32), jnp.zeros((sub, 2 * B_V), F32)) for _ in range(n_sub))
    res = lax.fori_loop(0, seq // tk, body, init, unroll=True)
    for j in range(n_sub):
        acc = res[j][1]
        o_ref[0, j * sub:(j + 1) * sub, :] = (acc[:, :B_V] / acc[:, B_V:]).astype(BF16)


def _mla_attention(qb, kb, vb, tq, tk, n_sub):
    B, H, S, _ = qb.shape
    return pl.pallas_call(
        functools.partial(_mla_kernel, tk=tk, n_sub=n_sub),
        grid=(B, H, S // tq),
        in_specs=[
            pl.BlockSpec((1, 1, tq, B_QK), lambda b, h, i: (b, h, i, 0)),
            pl.BlockSpec((1, 1, S, B_QK), lambda b, h, i: (b, h, 0, 0)),
            pl.BlockSpec((1, 1, S, B_V), lambda b, h, i: (b, h, 0, 0)),
        ],
        out_specs=pl.BlockSpec((1, tq, B_V), lambda b, h, i: (b, i, h)),
        out_shape=jax.ShapeDtypeStruct((B, S, H * B_V), BF16),
        compiler_params=_params("arbitrary", "arbitrary", "arbitrary"),
        name="mla_attn",
    )(qb, kb, vb)


def _mix_out_kernel(*refs):
    nb = len(DILATIONS)
    x_ref = refs[0]
    acc_refs, st_refs = refs[1:1 + nb], refs[1 + nb:1 + 2 * nb]
    ob_ref, mkv_ref, exp_ref, ga_ref, gb_ref, wout_ref, nq_ref, wmq_ref, wmo_ref, o_ref = refs[1 + 2 * nb:]

    m = [st[0, :, :LANES] for st in st_refs]
    l = [st[0, :, LANES:] for st in st_refs]
    m_all = functools.reduce(jnp.maximum, m)
    scale = [jnp.exp2(mb - m_all) for mb in m]
    inv = 1.0 / sum(sb * lb for sb, lb in zip(scale, l))
    oa = 0.0
    for sb, acc_ref in zip(scale, acc_refs):
        hi, lo = _split_bf16(sb * inv)
        oa = oa + (_dot(hi, exp_ref[...]) + _dot(lo, exp_ref[...])) * acc_ref[0].astype(F32)

    oa = _rms(oa, ga_ref[...]).astype(BF16)
    ob = _rms(ob_ref[0], gb_ref[...]).astype(BF16)
    x1 = x_ref[0] + _dot(oa, wout_ref[0:A_WIDTH, :]) + _dot(ob, wout_ref[A_WIDTH:, :])

    mq = (_dot(_rms(x1, nq_ref[...]).astype(BF16), wmq_ref[...]) * _MQ_SCALE).astype(BF16)
    outs = []
    for hd in range(M_HEADS):
        c = slice(hd * M_HEAD_DIM, (hd + 1) * M_HEAD_DIM)
        s = _dot_nt(mq[:, c], mkv_ref[0, :, c])
        p = jnp.exp2(s - jnp.max(s, axis=-1, keepdims=True))
        den = jnp.sum(p, axis=-1, keepdims=True)
        cv = slice(M_WIDTH + hd * M_HEAD_DIM, M_WIDTH + (hd + 1) * M_HEAD_DIM)
        outs.append(_dot(p.astype(BF16), mkv_ref[0, :, cv]) / den)
    mo = jnp.concatenate(outs, axis=1).astype(BF16)
    o_ref[0] = x1 + _dot(mo, wmo_ref[...])


def _mix_out(x, branches, ob, mkv, expand, ga, gb, wout, nq, wmq, wmo, tm):
    B, S, D = x.shape
    const = lambda shape: pl.BlockSpec(shape, lambda b, i: (0,) * len(shape))
    tok = lambda n: pl.BlockSpec((1, tm, n), lambda b, i: (b, i, 0))
    accs = [a for a, _ in branches]
    stats = [s for _, s in branches]
    return pl.pallas_call(
        _mix_out_kernel,
        grid=(B, S // tm),
        in_specs=[tok(D)] + [tok(A_WIDTH)] * len(accs) + [tok(2 * LANES)] * len(stats)
        + [tok(B_WIDTH), pl.BlockSpec((1, N_MEM, 2 * M_WIDTH), lambda b, i: (b, 0, 0)),
           const(expand.shape), const((1, A_WIDTH)), const((1, B_WIDTH)), const(wout.shape), const((1, D)),
           const(wmq.shape), const(wmo.shape)],
        out_specs=tok(D),
        out_shape=jax.ShapeDtypeStruct((B, S, D), F32),
        compiler_params=_params("arbitrary", "arbitrary"),
        name="mix_out",
    )(x, *accs, *stats, ob, mkv, expand, ga, gb, wout, nq, wmq, wmo)


def _ffn_kernel(x_ref, nf_ref, wg_ref, wu_ref, wd_ref, nfin_ref, o_ref):
    x = x_ref[...]
    hf = _rms(x, nf_ref[...]).astype(BF16)
    g = _dot(hf, wg_ref[...])
    u = _dot(hf, wu_ref[...])
    a = (g / (1.0 + jnp.exp(-g)) * u).astype(BF16)
    y = x + _dot(a, wd_ref[...])
    o_ref[...] = _rms(y, nfin_ref[...])


def _ffn(x, nf, wg, wu, wd, nfin, tm):
    T, D = x.shape
    const = lambda shape: pl.BlockSpec(shape, lambda i: (0,) * len(shape), pipeline_mode=pl.Buffered(1))
    tok = pl.BlockSpec((tm, D), lambda i: (i, 0))
    return pl.pallas_call(
        _ffn_kernel,
        grid=(T // tm,),
        in_specs=[tok, const((1, D)), const(wg.shape), const(wu.shape), const(wd.shape), const((1, D))],
        out_specs=tok,
        out_shape=jax.ShapeDtypeStruct((T, D), F32),
        compiler_params=_params("arbitrary"),
        name="ffn",
    )(x, nf, wg, wu, wd, nfin)


def _rotate_half_cols(w):
    half = w.shape[-1] // 2
    return jnp.concatenate([-w[..., half:], w[..., :half]], axis=-1)


def kernel(x, mem, positions, norm_mix, w_in, q_norm, w_q_up, kv_norm, w_kv_up, gout_a, gout_b, w_out,
           norm_mem_q, norm_mem_kv, w_mq, w_mkv, w_mo, norm_ffn, w_gate, w_up, w_down, norm_final):
    B, S, D = x.shape
    depth = w_in.shape[0]
    half = B_ROPE // 2
    inv_freq = ROPE_THETA ** (-jnp.arange(half, dtype=F32) / half)
    invf = jnp.tile(inv_freq, LANES // half)[None, :]
    pos_col = positions[:, :, None]
    perms = {dil: _class_perm(dil) for dil in _REGROUPED}
    expand = np.zeros((LANES, A_WIDTH), np.float32)
    for hd in range(A_HEADS):
        expand[hd, hd * A_HEAD_DIM:(hd + 1) * A_HEAD_DIM] = 1.0

    for l in range(depth):
        w_kr = w_in[l][:, _C_KR:]
        win = jnp.concatenate([w_in[l], _rotate_half_cols(w_kr)], axis=1).astype(BF16)
        wq3 = w_q_up[l].reshape(Q_LORA, B_HEADS, B_QK)
        wq_pe = wq3[:, :, B_NOPE:]
        wq = jnp.concatenate([wq3[:, :, :B_NOPE].reshape(Q_LORA, -1), wq_pe.reshape(Q_LORA, -1),
                              _rotate_half_cols(wq_pe).reshape(Q_LORA, -1)], axis=1).astype(BF16)

        qkv, (qb, kb, vb) = _in_proj(x, pos_col, invf, norm_mix[l][None], win, q_norm[l][None], wq,
                                     kv_norm[l][None], w_kv_up[l].astype(BF16),
                                     [jnp.asarray(perms[dil], BF16) for dil in _REGROUPED], tm=512)

        branches = [_dilated_branch(*qkv[dil], positions, dil,
                                    None if dil == 1 else jnp.asarray(perms[dil].T, BF16))
                    for dil in DILATIONS]

        ob = _mla_attention(qb, kb, vb, tq=1024, tk=512, n_sub=4)

        mkv = _mem_kv(mem, norm_mem_kv[l][None], w_mkv[l].astype(BF16))
        x = _mix_out(x, branches, ob, mkv, jnp.asarray(expand, BF16), gout_a[l][None], gout_b[l][None],
                     w_out[l].astype(BF16), norm_mem_q[l][None], w_mq[l].astype(BF16), w_mo[l].astype(BF16),
                     tm=512)

        last_norm = norm_final[None] if l == depth - 1 else None
        assert last_norm is not None, "the ffn call applies the final norm; depth is 1 for this problem"
        x = _ffn(x.reshape(B * S, D), norm_ffn[l][None], w_gate[l].astype(BF16), w_up[l].astype(BF16),
                 w_down[l].astype(BF16), last_norm, tm=512).reshape(B, S, D)
    return x
```

```python
import functools
import math

import jax
import jax.numpy as jnp
import numpy as np
from jax import lax
from jax.experimental import pallas as pl
from jax.experimental.pallas import tpu as pltpu

F32 = jnp.float32
BF16 = jnp.bfloat16

EPS = 1e-6
D_MODEL = 1024
N_MEM = 256

A_HEADS = 8
A_HEAD_DIM = 64
A_WIDTH = A_HEADS * A_HEAD_DIM
DILATIONS = (1, 4, 16)
HALF_SPAN = 64
Q_TILE = 128
K_WIN = Q_TILE + 2 * HALF_SPAN
PERM = 256
_DILATED_TILING = {1: (8, 1), 4: (2, 4), 16: (1, 4)}

B_HEADS = 4
B_NOPE = 128
B_ROPE = 64
B_QK = B_NOPE + B_ROPE
B_V = 128
B_WIDTH = B_HEADS * B_V
Q_LORA = 384
KV_LORA = 256
ROPE_THETA = 10000.0

M_HEADS = 4
M_HEAD_DIM = 128
M_WIDTH = M_HEADS * M_HEAD_DIM

LOG2E = math.log2(math.e)
MASK_DIST = 1e30
LANES = 128
VMEM_LIMIT = 56 * 1024 * 1024


def _rms(x, g):
    x = x.astype(F32)
    return x * lax.rsqrt(jnp.mean(x * x, axis=-1, keepdims=True) + EPS) * g


def _dot(a, b):
    return jnp.dot(a, b, preferred_element_type=F32)


def _dot_nt(a, b):
    return lax.dot_general(a, b, (((1,), (1,)), ((), ())), preferred_element_type=F32)


def _split_bf16(x):
    hi = x.astype(BF16)
    return hi, (x - hi.astype(F32)).astype(BF16)


def _params(*sem):
    return pltpu.CompilerParams(dimension_semantics=sem, vmem_limit_bytes=VMEM_LIMIT)


def _class_perm(dil):
    idx = np.arange(PERM)
    r, a = idx // (PERM // dil), idx % (PERM // dil)
    p = np.zeros((PERM, PERM), np.float32)
    p[idx, dil * a + r] = 1.0
    return p


def _mem_kv_kernel(mem_ref, g_ref, w_ref, o_ref):
    o_ref[0] = _dot(_rms(mem_ref[0], g_ref[...]).astype(BF16), w_ref[...]).astype(BF16)


def _mem_kv(mem, g, w):
    B, M, D = mem.shape
    N = w.shape[1]
    return pl.pallas_call(
        _mem_kv_kernel,
        grid=(B,),
        in_specs=[
            pl.BlockSpec((1, M, D), lambda b: (b, 0, 0)),
            pl.BlockSpec((1, D), lambda b: (0, 0)),
            pl.BlockSpec((D, N), lambda b: (0, 0)),
        ],
        out_specs=pl.BlockSpec((1, M, N), lambda b: (b, 0, 0)),
        out_shape=jax.ShapeDtypeStruct((B, M, N), BF16),
        compiler_params=_params("arbitrary"),
        name="mem_kv",
    )(mem, g, w)


_C_QA, _C_KA, _C_VA, _C_CQ, _C_CKV, _C_KR, _C_END = 0, 512, 1024, 1536, 1920, 2176, 2304
_QA_SCALE = A_HEAD_DIM ** -0.5 * LOG2E
_QB_SCALE = B_QK ** -0.5 * LOG2E
_MQ_SCALE = M_HEAD_DIM ** -0.5 * LOG2E
_REGROUPED = DILATIONS[1:]


def _in_proj_kernel(x_ref, pos_ref, invf_ref, nmix_ref, win_ref, qn_ref, wq_ref, kvn_ref, wkv_ref, *refs):
    perm_refs = refs[:len(_REGROUPED)]
    qa_ref, ka_ref, va_ref, qb_ref, kb_ref, vb_ref = refs[len(_REGROUPED):len(_REGROUPED) + 6]
    class_refs = refs[len(_REGROUPED) + 6:]
    tm = x_ref.shape[1]
    h = _rms(x_ref[0], nmix_ref[...]).astype(BF16)

    def seg(a, b):
        return _dot(h, win_ref[:, a:b])

    qkv = [(seg(_C_QA, _C_KA) * _QA_SCALE).astype(BF16), seg(_C_KA, _C_VA).astype(BF16),
           seg(_C_VA, _C_CQ).astype(BF16)]
    for val, ref in zip(qkv, (qa_ref, ka_ref, va_ref)):
        ref[0, 0] = val
    for n, dil in enumerate(_REGROUPED):
        per = PERM // dil
        for part in range(tm // PERM):
            for val, ref in zip(qkv, class_refs[3 * n:3 * n + 3]):
                y = _dot(perm_refs[n][...], val[part * PERM:(part + 1) * PERM])
                ref[0, :, part * per:(part + 1) * per, :] = y.reshape(dil, per, A_WIDTH).astype(BF16)

    cq = seg(_C_CQ, _C_CKV)
    ckv = seg(_C_CKV, _C_KR)
    kr2 = seg(_C_KR, _C_END)

    ang = pos_ref[0].astype(F32) * invf_ref[...]
    cos4 = jnp.cos(ang)
    sin4 = jnp.sin(ang)
    cos8 = jnp.concatenate([cos4, cos4], axis=1)
    sin8 = jnp.concatenate([sin4, sin4], axis=1)

    qb = _dot(_rms(cq, qn_ref[...]).astype(BF16), wq_ref[...])
    n0 = B_HEADS * B_NOPE
    n1 = n0 + B_HEADS * B_ROPE
    q_pe = (qb[:, n0:n1] * cos8 + qb[:, n1:] * sin8) * _QB_SCALE

    lane = lax.broadcasted_iota(jnp.int32, kr2.shape, 1)
    t = kr2 * jnp.where(lane < B_ROPE, cos4, sin4)
    k_pe = (t[:, :B_ROPE] + t[:, B_ROPE:]).astype(BF16)

    kvb = _dot(_rms(ckv, kvn_ref[...]).astype(BF16), wkv_ref[...])
    for hd in range(B_HEADS):
        qb_ref[0, hd, :, 0:B_NOPE] = (qb[:, hd * B_NOPE:(hd + 1) * B_NOPE] * _QB_SCALE).astype(BF16)
        qb_ref[0, hd, :, B_NOPE:B_QK] = q_pe[:, hd * B_ROPE:(hd + 1) * B_ROPE].astype(BF16)
        c0 = hd * (B_NOPE + B_V)
        kb_ref[0, hd, :, 0:B_NOPE] = kvb[:, c0:c0 + B_NOPE].astype(BF16)
        kb_ref[0, hd, :, B_NOPE:B_QK] = k_pe
        vb_ref[0, hd] = kvb[:, c0 + B_NOPE:c0 + B_NOPE + B_V].astype(BF16)


def _in_proj(x, pos_col, invf, nmix, win, qn, wq, kvn, wkv, perms, tm):
    B, S, D = x.shape
    const = lambda shape: pl.BlockSpec(shape, lambda b, i: (0,) * len(shape))
    tok = lambda n: pl.BlockSpec((1, tm, n), lambda b, i: (b, i, 0))
    head = lambda n: pl.BlockSpec((1, B_HEADS, tm, n), lambda b, i: (b, 0, i, 0))
    cls = lambda dil: pl.BlockSpec((1, dil, tm // dil, A_WIDTH), lambda b, i: (b, 0, i, 0))
    out_specs = [cls(1)] * 3 + [head(B_QK), head(B_QK), head(B_V)]
    out_shape = ([jax.ShapeDtypeStruct((B, 1, S, A_WIDTH), BF16)] * 3
                 + [jax.ShapeDtypeStruct((B, B_HEADS, S, B_QK), BF16)] * 2
                 + [jax.ShapeDtypeStruct((B, B_HEADS, S, B_V), BF16)])
    for dil in _REGROUPED:
        out_specs += [cls(dil)] * 3
        out_shape += [jax.ShapeDtypeStruct((B, dil, S // dil, A_WIDTH), BF16)] * 3
    res = pl.pallas_call(
        _in_proj_kernel,
        grid=(B, S // tm),
        in_specs=[tok(D), tok(1), const((1, LANES)), const((1, D)), const(win.shape),
                  const((1, Q_LORA)), const(wq.shape), const((1, KV_LORA)), const(wkv.shape)]
        + [const((PERM, PERM))] * len(perms),
        out_specs=out_specs,
        out_shape=out_shape,
        compiler_params=_params("arbitrary", "arbitrary"),
        name="in_proj",
    )(x, pos_col, invf, nmix, win, qn, wq, kvn, wkv, *perms)
    qkv = {1: tuple(res[0:3])}
    for n, dil in enumerate(_REGROUPED):
        qkv[dil] = tuple(res[6 + 3 * n:9 + 3 * n])
    return qkv, res[3:6]


def _tile_attention(q, kw, vw, dist):
    lane = lax.broadcasted_iota(jnp.int32, (Q_TILE, LANES), 1)
    low = lane < A_HEAD_DIM
    ones = jnp.ones((K_WIN, LANES), BF16)
    m_tile = jnp.zeros((Q_TILE, LANES), F32)
    l_tile = jnp.ones((Q_TILE, LANES), F32)
    pairs = []
    for pair in range(A_HEADS // 2):
        c = slice(pair * LANES, (pair + 1) * LANES)
        qp, kp = q[:, c], kw[:, c]
        v_ext = jnp.concatenate([vw[:, c], ones], axis=1)
        acc = []
        for half in range(2):
            hd = 2 * pair + half
            qm = jnp.where(low if half == 0 else ~low, qp, jnp.zeros_like(qp))
            s = _dot_nt(qm, kp) - (2.0 ** -(hd + 1) * LOG2E) * dist
            m = jnp.max(s, axis=-1, keepdims=True)
            r = _dot(jnp.exp2(s - m).astype(BF16), v_ext)
            acc.append(r[:, :LANES])
            m_tile = jnp.where(lane == hd, m, m_tile)
            l_tile = jnp.where(lane == hd, r[:, LANES:], l_tile)
        pairs.append(jnp.where(low, acc[0], acc[1]))
    return jnp.concatenate(pairs, axis=1), jnp.concatenate([m_tile, l_tile], axis=1)


def _dilated_kernel(*refs, dil, seq, tps, unroll):
    if dil == 1:
        q_ref, k_ref, v_ref, pq_ref, pk_ref, acc_ref, st_ref = refs
    else:
        q_ref, k_ref, v_ref, pq_ref, pk_ref, pt_ref, acc_ref, st_ref, acc_scr, st_scr = refs
    step = pl.program_id(1)

    def tile(cls, t_local):
        start = (step * tps + t_local) * Q_TILE
        if seq == K_WIN:
            win0 = 0
        else:
            win0 = pl.multiple_of(jnp.clip(start - HALF_SPAN, 0, seq - K_WIN), HALF_SPAN)
        rows = pl.ds(t_local * Q_TILE, Q_TILE)
        dist = jnp.abs(pq_ref[0, cls, t_local] - pk_ref[0, t_local, cls]).astype(F32)
        qi = start + lax.broadcasted_iota(jnp.int32, (Q_TILE, 1), 0)
        ki = win0 + lax.broadcasted_iota(jnp.int32, (1, K_WIN), 1)
        dist = jnp.where(jnp.abs(ki - qi) <= HALF_SPAN, dist, MASK_DIST)
        return _tile_attention(q_ref[0, cls, rows, :], k_ref[0, cls, pl.ds(win0, K_WIN), :],
                               v_ref[0, cls, pl.ds(win0, K_WIN), :], dist)

    if dil == 1:
        for t in range(tps):
            acc, st = tile(0, t)
            acc_ref[0, t * Q_TILE:(t + 1) * Q_TILE, :] = acc.astype(BF16)
            st_ref[0, t * Q_TILE:(t + 1) * Q_TILE, :] = st
        return

    def body(cls, carry):
        for t in range(tps):
            acc, st = tile(cls, t)
            acc_scr[cls, t * Q_TILE:(t + 1) * Q_TILE, :] = acc.astype(BF16)
            st_scr[cls, t * Q_TILE:(t + 1) * Q_TILE, :] = st
        return carry

    if unroll >= dil:
        for cls in range(dil):
            body(cls, 0)
    else:
        lax.fori_loop(0, dil, body, 0, unroll=unroll)

    per = PERM // dil
    for part in range(dil * tps * Q_TILE // PERM):
        rows = slice(part * per, (part + 1) * per)
        out = slice(part * PERM, (part + 1) * PERM)
        acc = jnp.concatenate([acc_scr[cls, rows, :] for cls in range(dil)], axis=0)
        acc_ref[0, out, :] = _dot(pt_ref[...], acc).astype(BF16)
        hi, lo = _split_bf16(jnp.concatenate([st_scr[cls, rows, :] for cls in range(dil)], axis=0))
        st_ref[0, out, :] = _dot(pt_ref[...], hi) + _dot(pt_ref[...], lo)


def _dilated_branch(q, k, v, positions, dil, perm_t):
    B, _, seq, W = q.shape
    S = seq * dil
    nblk = seq // Q_TILE
    tps, unroll = _DILATED_TILING[dil]
    pos_cls = positions.reshape(B, seq, dil).transpose(0, 2, 1)
    pos_q = pos_cls.reshape(B, dil, nblk, Q_TILE, 1)
    starts = [min(max(i * Q_TILE - HALF_SPAN, 0), seq - K_WIN) for i in range(nblk)]
    pos_k = jnp.stack([pos_cls[:, :, s:s + K_WIN] for s in starts], axis=1)[:, :, :, None, :]

    whole = pl.BlockSpec((1, dil, seq, W), lambda b, i: (b, 0, 0, 0))
    in_specs = [pl.BlockSpec((1, dil, tps * Q_TILE, W), lambda b, i: (b, 0, i, 0)), whole, whole,
                pl.BlockSpec((1, dil, tps, Q_TILE, 1), lambda b, i: (b, 0, i, 0, 0)),
                pl.BlockSpec((1, tps, dil, 1, K_WIN), lambda b, i: (b, i, 0, 0, 0))]
    args = [q, k, v, pos_q, pos_k]
    scratch = []
    if dil > 1:
        in_specs.append(pl.BlockSpec((PERM, PERM), lambda b, i: (0, 0)))
        args.append(perm_t)
        scratch = [pltpu.VMEM((dil, tps * Q_TILE, W), BF16), pltpu.VMEM((dil, tps * Q_TILE, 2 * LANES), F32)]
    chunk = dil * tps * Q_TILE
    return pl.pallas_call(
        functools.partial(_dilated_kernel, dil=dil, seq=seq, tps=tps, unroll=unroll),
        grid=(B, S // chunk),
        in_specs=in_specs,
        out_specs=[pl.BlockSpec((1, chunk, W), lambda b, i: (b, i, 0)),
                   pl.BlockSpec((1, chunk, 2 * LANES), lambda b, i: (b, i, 0))],
        out_shape=[jax.ShapeDtypeStruct((B, S, W), BF16), jax.ShapeDtypeStruct((B, S, 2 * LANES), F32)],
        scratch_shapes=scratch,
        compiler_params=_params("arbitrary", "arbitrary"),
        name=f"dilated_{dil}",
    )(*args)


def _mla_kernel(q_ref, k_ref, v_ref, o_ref, *, tk, n_sub):
    tq = q_ref.shape[2]
    sub = tq // n_sub
    seq = k_ref.shape[2]
    ones = jnp.ones((tk, B_V), BF16)

    def body(c, carry):
        off = pl.multiple_of(c * tk, tk)
        kc = k_ref[0, 0, pl.ds(off, tk), :]
        vc = jnp.concatenate([v_ref[0, 0, pl.ds(off, tk), :], ones], axis=1)
        new = []
        for j in range(n_sub):
            m, acc = carry[j]
            s = _dot_nt(q_ref[0, 0, j * sub:(j + 1) * sub, :], kc)
            m_new = jnp.maximum(m, jnp.max(s, axis=-1, keepdims=True))
            p = jnp.exp2(s - m_new).astype(BF16)
            acc = jnp.exp2(m - m_new) * acc + _dot(p, vc)
            new.append((m_new, acc))
        return tuple(new)

    init = tuple((jnp.full((sub, 1), -jnp.inf, F32), jnp.zeros((sub, 2 * B_V), F32)) for _ in range(n_sub))
    res = lax.fori_loop(0, seq // tk, body, init, unroll=True)
    for j in range(n_sub):
        acc = res[j][1]
        o_ref[0, j * sub:(j + 1) * sub, :] = (acc[:, :B_V] / acc[:, B_V:]).astype(BF16)


def _mla_attention(qb, kb, vb, tq, tk, n_sub):
    B, H, S, _ = qb.shape
    return pl.pallas_call(
        functools.partial(_mla_kernel, tk=tk, n_sub=n_sub),
        grid=(B, H, S // tq),
        in_specs=[
            pl.BlockSpec((1, 1, tq, B_QK), lambda b, h, i: (b, h, i, 0)),
            pl.BlockSpec((1, 1, S, B_QK), lambda b, h, i: (b, h, 0, 0)),
            pl.BlockSpec((1, 1, S, B_V), lambda b, h, i: (b, h, 0, 0)),
        ],
        out_specs=pl.BlockSpec((1, tq, B_V), lambda b, h, i: (b, i, h)),
        out_shape=jax.ShapeDtypeStruct((B, S, H * B_V), BF16),
        compiler_params=_params("arbitrary", "arbitrary", "arbitrary"),
        name="mla_attn",
    )(qb, kb, vb)


def _mix_out_kernel(*refs):
    nb = len(DILATIONS)
    x_ref = refs[0]
    acc_refs, st_refs = refs[1:1 + nb], refs[1 + nb:1 + 2 * nb]
    ob_ref, mkv_ref, exp_ref, ga_ref, gb_ref, wout_ref, nq_ref, wmq_ref, wmo_ref, o_ref = refs[1 + 2 * nb:]

    m = [st[0, :, :LANES] for st in st_refs]
    l = [st[0, :, LANES:] for st in st_refs]
    m_all = functools.reduce(jnp.maximum, m)
    scale = [jnp.exp2(mb - m_all) for mb in m]
    inv = 1.0 / sum(sb * lb for sb, lb in zip(scale, l))
    oa = 0.0
    for sb, acc_ref in zip(scale, acc_refs):
        w = _dot(jnp.concatenate(_split_bf16(sb * inv), axis=1), exp_ref[...])
        oa = oa + w * acc_ref[0].astype(F32)

    oa = _rms(oa, ga_ref[...]).astype(BF16)
    ob = _rms(ob_ref[0], gb_ref[...]).astype(BF16)
    x1 = x_ref[0] + _dot(oa, wout_ref[0:A_WIDTH, :]) + _dot(ob, wout_ref[A_WIDTH:, :])

    mq = (_dot(_rms(x1, nq_ref[...]).astype(BF16), wmq_ref[...]) * _MQ_SCALE).astype(BF16)
    outs = []
    ones = jnp.ones((N_MEM, M_HEAD_DIM), BF16)
    for hd in range(M_HEADS):
        c = slice(hd * M_HEAD_DIM, (hd + 1) * M_HEAD_DIM)
        s = _dot_nt(mq[:, c], mkv_ref[0, :, c])
        p = jnp.exp2(s - jnp.max(s, axis=-1, keepdims=True)).astype(BF16)
        cv = slice(M_WIDTH + hd * M_HEAD_DIM, M_WIDTH + (hd + 1) * M_HEAD_DIM)
        r = _dot(p, jnp.concatenate([mkv_ref[0, :, cv], ones], axis=1))
        outs.append(r[:, :M_HEAD_DIM] / r[:, M_HEAD_DIM:])
    mo = jnp.concatenate(outs, axis=1).astype(BF16)
    o_ref[0] = x1 + _dot(mo, wmo_ref[...])


def _mix_out(x, branches, ob, mkv, expand, ga, gb, wout, nq, wmq, wmo, tm):
    B, S, D = x.shape
    const = lambda shape: pl.BlockSpec(shape, lambda b, i: (0,) * len(shape))
    tok = lambda n: pl.BlockSpec((1, tm, n), lambda b, i: (b, i, 0))
    accs = [a for a, _ in branches]
    stats = [s for _, s in branches]
    return pl.pallas_call(
        _mix_out_kernel,
        grid=(B, S // tm),
        in_specs=[tok(D)] + [tok(A_WIDTH)] * len(accs) + [tok(2 * LANES)] * len(stats)
        + [tok(B_WIDTH), pl.BlockSpec((1, N_MEM, 2 * M_WIDTH), lambda b, i: (b, 0, 0)),
           const(expand.shape), const((1, A_WIDTH)), const((1, B_WIDTH)), const(wout.shape), const((1, D)),
           const(wmq.shape), const(wmo.shape)],
        out_specs=tok(D),
        out_shape=jax.ShapeDtypeStruct((B, S, D), F32),
        compiler_params=_params("arbitrary", "arbitrary"),
        name="mix_out",
    )(x, *accs, *stats, ob, mkv, expand, ga, gb, wout, nq, wmq, wmo)


def _ffn_kernel(x_ref, nf_ref, wg_ref, wu_ref, wd_ref, nfin_ref, o_ref):
    x = x_ref[...]
    hf = _rms(x, nf_ref[...]).astype(BF16)
    g = _dot(hf, wg_ref[...])
    u = _dot(hf, wu_ref[...])
    a = (g / (1.0 + jnp.exp(-g)) * u).astype(BF16)
    y = x + _dot(a, wd_ref[...])
    o_ref[...] = _rms(y, nfin_ref[...])


def _ffn(x, nf, wg, wu, wd, nfin, tm):
    T, D = x.shape
    const = lambda shape: pl.BlockSpec(shape, lambda i: (0,) * len(shape), pipeline_mode=pl.Buffered(1))
    tok = pl.BlockSpec((tm, D), lambda i: (i, 0))
    return pl.pallas_call(
        _ffn_kernel,
        grid=(T // tm,),
        in_specs=[tok, const((1, D)), const(wg.shape), const(wu.shape), const(wd.shape), const((1, D))],
        out_specs=tok,
        out_shape=jax.ShapeDtypeStruct((T, D), F32),
        compiler_params=_params("arbitrary"),
        name="ffn",
    )(x, nf, wg, wu, wd, nfin)


def _rotate_half_cols(w):
    half = w.shape[-1] // 2
    return jnp.concatenate([-w[..., half:], w[..., :half]], axis=-1)


def kernel(x, mem, positions, norm_mix, w_in, q_norm, w_q_up, kv_norm, w_kv_up, gout_a, gout_b, w_out,
           norm_mem_q, norm_mem_kv, w_mq, w_mkv, w_mo, norm_ffn, w_gate, w_up, w_down, norm_final):
    B, S, D = x.shape
    depth = w_in.shape[0]
    half = B_ROPE // 2
    inv_freq = ROPE_THETA ** (-jnp.arange(half, dtype=F32) / half)
    invf = jnp.tile(inv_freq, LANES // half)[None, :]
    pos_col = positions[:, :, None]
    perms = {dil: _class_perm(dil) for dil in _REGROUPED}
    expand = np.zeros((2 * LANES, A_WIDTH), np.float32)
    for hd in range(A_HEADS):
        expand[[hd, LANES + hd], hd * A_HEAD_DIM:(hd + 1) * A_HEAD_DIM] = 1.0

    for l in range(depth):
        w_kr = w_in[l][:, _C_KR:]
        win = jnp.concatenate([w_in[l], _rotate_half_cols(w_kr)], axis=1).astype(BF16)
        wq3 = w_q_up[l].reshape(Q_LORA, B_HEADS, B_QK)
        wq_pe = wq3[:, :, B_NOPE:]
        wq = jnp.concatenate([wq3[:, :, :B_NOPE].reshape(Q_LORA, -1), wq_pe.reshape(Q_LORA, -1),
                              _rotate_half_cols(wq_pe).reshape(Q_LORA, -1)], axis=1).astype(BF16)

        qkv, (qb, kb, vb) = _in_proj(x, pos_col, invf, norm_mix[l][None], win, q_norm[l][None], wq,
                                     kv_norm[l][None], w_kv_up[l].astype(BF16),
                                     [jnp.asarray(perms[dil], BF16) for dil in _REGROUPED], tm=512)

        branches = [_dilated_branch(*qkv[dil], positions, dil,
                                    None if dil == 1 else jnp.asarray(perms[dil].T, BF16))
                    for dil in DILATIONS]

        ob = _mla_attention(qb, kb, vb, tq=1024, tk=512, n_sub=4)

        mkv = _mem_kv(mem, norm_mem_kv[l][None], w_mkv[l].astype(BF16))
        x = _mix_out(x, branches, ob, mkv, jnp.asarray(expand, BF16), gout_a[l][None], gout_b[l][None],
                     w_out[l].astype(BF16), norm_mem_q[l][None], w_mq[l].astype(BF16), w_mo[l].astype(BF16),
                     tm=512)

        last_norm = norm_final[None] if l == depth - 1 else None
        assert last_norm is not None, "the ffn call applies the final norm; depth is 1 for this problem"
        x = _ffn(x.reshape(B * S, D), norm_ffn[l][None], w_gate[l].astype(BF16), w_up[l].astype(BF16),
                 w_down[l].astype(BF16), last_norm, tm=512).reshape(B, S, D)
    return x
```

```python
import functools
import math

import jax
import jax.numpy as jnp
import numpy as np
from jax import lax
from jax.experimental import pallas as pl
from jax.experimental.pallas import tpu as pltpu

F32 = jnp.float32
BF16 = jnp.bfloat16

EPS = 1e-6
D_MODEL = 1024
N_MEM = 256

A_HEADS = 8
A_HEAD_DIM = 64
A_WIDTH = A_HEADS * A_HEAD_DIM
DILATIONS = (1, 4, 16)
HALF_SPAN = 64
Q_TILE = 128
K_WIN = Q_TILE + 2 * HALF_SPAN
PERM = 256
_DILATED_TILING = {1: (8, 1), 4: (2, 4), 16: (1, 4)}

B_HEADS = 4
B_NOPE = 128
B_ROPE = 64
B_QK = B_NOPE + B_ROPE
B_V = 128
B_WIDTH = B_HEADS * B_V
Q_LORA = 384
KV_LORA = 256
ROPE_THETA = 10000.0

M_HEADS = 4
M_HEAD_DIM = 128
M_WIDTH = M_HEADS * M_HEAD_DIM

LOG2E = math.log2(math.e)
MASK_DIST = 1e30
LANES = 128
VMEM_LIMIT = 56 * 1024 * 1024


def _rms(x, g):
    x = x.astype(F32)
    return x * lax.rsqrt(jnp.mean(x * x, axis=-1, keepdims=True) + EPS) * g


def _dot(a, b):
    return jnp.dot(a, b, preferred_element_type=F32)


def _dot_nt(a, b):
    return lax.dot_general(a, b, (((1,), (1,)), ((), ())), preferred_element_type=F32)


def _split_bf16(x):
    hi = x.astype(BF16)
    return hi, (x - hi.astype(F32)).astype(BF16)


def _row_to_col(row):
    n = row.shape[1]
    diag = lax.broadcasted_iota(jnp.int32, (n, n), 0) == lax.broadcasted_iota(jnp.int32, (n, n), 1)
    return jnp.sum(jnp.where(diag, row, 0), axis=1, keepdims=True)


def _params(*sem):
    return pltpu.CompilerParams(dimension_semantics=sem, vmem_limit_bytes=VMEM_LIMIT)


def _class_perm(dil):
    idx = np.arange(PERM)
    r, a = idx // (PERM // dil), idx % (PERM // dil)
    p = np.zeros((PERM, PERM), np.float32)
    p[idx, dil * a + r] = 1.0
    return p


def _mem_kv_kernel(mem_ref, g_ref, w_ref, o_ref):
    o_ref[0] = _dot(_rms(mem_ref[0], g_ref[...]).astype(BF16), w_ref[...]).astype(BF16)


def _mem_kv(mem, g, w):
    B, M, D = mem.shape
    N = w.shape[1]
    return pl.pallas_call(
        _mem_kv_kernel,
        grid=(B,),
        in_specs=[
            pl.BlockSpec((1, M, D), lambda b: (b, 0, 0)),
            pl.BlockSpec((1, D), lambda b: (0, 0)),
            pl.BlockSpec((D, N), lambda b: (0, 0)),
        ],
        out_specs=pl.BlockSpec((1, M, N), lambda b: (b, 0, 0)),
        out_shape=jax.ShapeDtypeStruct((B, M, N), BF16),
        compiler_params=_params("arbitrary"),
        name="mem_kv",
    )(mem, g, w)


_C_QA, _C_KA, _C_VA, _C_CQ, _C_CKV, _C_KR, _C_END = 0, 512, 1024, 1536, 1920, 2176, 2304
_QA_SCALE = A_HEAD_DIM ** -0.5 * LOG2E
_QB_SCALE = B_QK ** -0.5 * LOG2E
_MQ_SCALE = M_HEAD_DIM ** -0.5 * LOG2E
_REGROUPED = DILATIONS[1:]


def _in_proj_kernel(x_ref, pos_ref, invf_ref, nmix_ref, win_ref, qn_ref, wq_ref, kvn_ref, wkv_ref, *refs):
    perm_refs = refs[:len(_REGROUPED)]
    qa_ref, ka_ref, va_ref, qb_ref, kb_ref, vb_ref = refs[len(_REGROUPED):len(_REGROUPED) + 6]
    class_refs = refs[len(_REGROUPED) + 6:]
    tm = x_ref.shape[1]
    h = _rms(x_ref[0], nmix_ref[...]).astype(BF16)

    def seg(a, b):
        return _dot(h, win_ref[:, a:b])

    qkv = [(seg(_C_QA, _C_KA) * _QA_SCALE).astype(BF16), seg(_C_KA, _C_VA).astype(BF16),
           seg(_C_VA, _C_CQ).astype(BF16)]
    for val, ref in zip(qkv, (qa_ref, ka_ref, va_ref)):
        ref[0, 0] = val
    for n, dil in enumerate(_REGROUPED):
        per = PERM // dil
        for part in range(tm // PERM):
            for val, ref in zip(qkv, class_refs[3 * n:3 * n + 3]):
                y = _dot(perm_refs[n][...], val[part * PERM:(part + 1) * PERM])
                ref[0, :, part * per:(part + 1) * per, :] = y.reshape(dil, per, A_WIDTH).astype(BF16)

    cq = seg(_C_CQ, _C_CKV)
    ckv = seg(_C_CKV, _C_KR)
    kr2 = seg(_C_KR, _C_END)

    pos = jnp.concatenate([_row_to_col(pos_ref[0, g]) for g in range(tm // LANES)], axis=0)
    ang = pos.astype(F32) * invf_ref[...]
    cos4 = jnp.cos(ang)
    sin4 = jnp.sin(ang)
    cos8 = jnp.concatenate([cos4, cos4], axis=1)
    sin8 = jnp.concatenate([sin4, sin4], axis=1)

    qb = _dot(_rms(cq, qn_ref[...]).astype(BF16), wq_ref[...])
    n0 = B_HEADS * B_NOPE
    n1 = n0 + B_HEADS * B_ROPE
    q_pe = (qb[:, n0:n1] * cos8 + qb[:, n1:] * sin8) * _QB_SCALE

    lane = lax.broadcasted_iota(jnp.int32, kr2.shape, 1)
    t = kr2 * jnp.where(lane < B_ROPE, cos4, sin4)
    k_pe = (t[:, :B_ROPE] + t[:, B_ROPE:]).astype(BF16)

    kvb = _dot(_rms(ckv, kvn_ref[...]).astype(BF16), wkv_ref[...])
    for hd in range(B_HEADS):
        qb_ref[0, hd, :, 0:B_NOPE] = (qb[:, hd * B_NOPE:(hd + 1) * B_NOPE] * _QB_SCALE).astype(BF16)
        qb_ref[0, hd, :, B_NOPE:B_QK] = q_pe[:, hd * B_ROPE:(hd + 1) * B_ROPE].astype(BF16)
        c0 = hd * (B_NOPE + B_V)
        kb_ref[0, hd, :, 0:B_NOPE] = kvb[:, c0:c0 + B_NOPE].astype(BF16)
        kb_ref[0, hd, :, B_NOPE:B_QK] = k_pe
        vb_ref[0, hd] = kvb[:, c0 + B_NOPE:c0 + B_NOPE + B_V].astype(BF16)


def _in_proj(x, pos_rows, invf, nmix, win, qn, wq, kvn, wkv, perms, tm):
    B, S, D = x.shape
    const = lambda shape: pl.BlockSpec(shape, lambda b, i: (0,) * len(shape))
    tok = lambda n: pl.BlockSpec((1, tm, n), lambda b, i: (b, i, 0))
    head = lambda n: pl.BlockSpec((1, B_HEADS, tm, n), lambda b, i: (b, 0, i, 0))
    cls = lambda dil: pl.BlockSpec((1, dil, tm // dil, A_WIDTH), lambda b, i: (b, 0, i, 0))
    out_specs = [cls(1)] * 3 + [head(B_QK), head(B_QK), head(B_V)]
    out_shape = ([jax.ShapeDtypeStruct((B, 1, S, A_WIDTH), BF16)] * 3
                 + [jax.ShapeDtypeStruct((B, B_HEADS, S, B_QK), BF16)] * 2
                 + [jax.ShapeDtypeStruct((B, B_HEADS, S, B_V), BF16)])
    for dil in _REGROUPED:
        out_specs += [cls(dil)] * 3
        out_shape += [jax.ShapeDtypeStruct((B, dil, S // dil, A_WIDTH), BF16)] * 3
    res = pl.pallas_call(
        _in_proj_kernel,
        grid=(B, S // tm),
        in_specs=[tok(D), pl.BlockSpec((1, tm // LANES, 1, LANES), lambda b, i: (b, i, 0, 0)),
                  const((1, LANES)), const((1, D)), const(win.shape),
                  const((1, Q_LORA)), const(wq.shape), const((1, KV_LORA)), const(wkv.shape)]
        + [const((PERM, PERM))] * len(perms),
        out_specs=out_specs,
        out_shape=out_shape,
        compiler_params=_params("arbitrary", "arbitrary"),
        name="in_proj",
    )(x, pos_rows, invf, nmix, win, qn, wq, kvn, wkv, *perms)
    qkv = {1: tuple(res[0:3])}
    for n, dil in enumerate(_REGROUPED):
        qkv[dil] = tuple(res[6 + 3 * n:9 + 3 * n])
    return qkv, res[3:6]


def _tile_attention(q, kw, vw, dist):
    lane = lax.broadcasted_iota(jnp.int32, (Q_TILE, LANES), 1)
    low = lane < A_HEAD_DIM
    ones = jnp.ones((K_WIN, LANES), BF16)
    m_tile = jnp.zeros((Q_TILE, LANES), F32)
    l_tile = jnp.ones((Q_TILE, LANES), F32)
    pairs = []
    for pair in range(A_HEADS // 2):
        c = slice(pair * LANES, (pair + 1) * LANES)
        qp, kp = q[:, c], kw[:, c]
        v_ext = jnp.concatenate([vw[:, c], ones], axis=1)
        acc = []
        for half in range(2):
            hd = 2 * pair + half
            qm = jnp.where(low if half == 0 else ~low, qp, jnp.zeros_like(qp))
            s = _dot_nt(qm, kp) - (2.0 ** -(hd + 1) * LOG2E) * dist
            m = jnp.max(s, axis=-1, keepdims=True)
            r = _dot(jnp.exp2(s - m).astype(BF16), v_ext)
            acc.append(r[:, :LANES])
            m_tile = jnp.where(lane == hd, m, m_tile)
            l_tile = jnp.where(lane == hd, r[:, LANES:], l_tile)
        pairs.append(jnp.where(low, acc[0], acc[1]))
    return jnp.concatenate(pairs, axis=1), jnp.concatenate([m_tile, l_tile], axis=1)


def _dilated_kernel(*refs, dil, seq, tps, unroll):
    if dil == 1:
        q_ref, k_ref, v_ref, pq_ref, pk_ref, acc_ref, st_ref = refs
    else:
        q_ref, k_ref, v_ref, pq_ref, pk_ref, pt_ref, acc_ref, st_ref, acc_scr, st_scr = refs
    step = pl.program_id(1)

    def tile(cls, t_local):
        start = (step * tps + t_local) * Q_TILE
        if seq == K_WIN:
            win0 = 0
        else:
            win0 = pl.multiple_of(jnp.clip(start - HALF_SPAN, 0, seq - K_WIN), HALF_SPAN)
        rows = pl.ds(t_local * Q_TILE, Q_TILE)
        dist = jnp.abs(_row_to_col(pq_ref[0, cls, t_local]) - pk_ref[0, t_local, cls]).astype(F32)
        qi = start + lax.broadcasted_iota(jnp.int32, (Q_TILE, 1), 0)
        ki = win0 + lax.broadcasted_iota(jnp.int32, (1, K_WIN), 1)
        dist = jnp.where(jnp.abs(ki - qi) <= HALF_SPAN, dist, MASK_DIST)
        return _tile_attention(q_ref[0, cls, rows, :], k_ref[0, cls, pl.ds(win0, K_WIN), :],
                               v_ref[0, cls, pl.ds(win0, K_WIN), :], dist)

    if dil == 1:
        for t in range(tps):
            acc, st = tile(0, t)
            acc_ref[0, t * Q_TILE:(t + 1) * Q_TILE, :] = acc.astype(BF16)
            st_ref[0, t * Q_TILE:(t + 1) * Q_TILE, :] = st
        return

    def body(cls, carry):
        for t in range(tps):
            acc, st = tile(cls, t)
            acc_scr[cls, t * Q_TILE:(t + 1) * Q_TILE, :] = acc.astype(BF16)
            st_scr[cls, t * Q_TILE:(t + 1) * Q_TILE, :] = st
        return carry

    if unroll >= dil:
        for cls in range(dil):
            body(cls, 0)
    else:
        lax.fori_loop(0, dil, body, 0, unroll=unroll)

    per = PERM // dil
    for part in range(dil * tps * Q_TILE // PERM):
        rows = slice(part * per, (part + 1) * per)
        out = slice(part * PERM, (part + 1) * PERM)
        acc = jnp.concatenate([acc_scr[cls, rows, :] for cls in range(dil)], axis=0)
        acc_ref[0, out, :] = _dot(pt_ref[...], acc).astype(BF16)
        hi, lo = _split_bf16(jnp.concatenate([st_scr[cls, rows, :] for cls in range(dil)], axis=0))
        st_ref[0, out, :] = _dot(pt_ref[...], hi) + _dot(pt_ref[...], lo)


def _dilated_branch(q, k, v, positions, dil, perm_t):
    B, _, seq, W = q.shape
    S = seq * dil
    nblk = seq // Q_TILE
    tps, unroll = _DILATED_TILING[dil]
    pos_cls = positions.reshape(B, seq, dil).transpose(0, 2, 1)
    pos_q = pos_cls.reshape(B, dil, nblk, 1, Q_TILE)
    starts = [min(max(i * Q_TILE - HALF_SPAN, 0), seq - K_WIN) for i in range(nblk)]
    pos_k = jnp.stack([pos_cls[:, :, s:s + K_WIN] for s in starts], axis=1)[:, :, :, None, :]

    whole = pl.BlockSpec((1, dil, seq, W), lambda b, i: (b, 0, 0, 0))
    in_specs = [pl.BlockSpec((1, dil, tps * Q_TILE, W), lambda b, i: (b, 0, i, 0)), whole, whole,
                pl.BlockSpec((1, dil, tps, 1, Q_TILE), lambda b, i: (b, 0, i, 0, 0)),
                pl.BlockSpec((1, tps, dil, 1, K_WIN), lambda b, i: (b, i, 0, 0, 0))]
    args = [q, k, v, pos_q, pos_k]
    scratch = []
    if dil > 1:
        in_specs.append(pl.BlockSpec((PERM, PERM), lambda b, i: (0, 0)))
        args.append(perm_t)
        scratch = [pltpu.VMEM((dil, tps * Q_TILE, W), BF16), pltpu.VMEM((dil, tps * Q_TILE, 2 * LANES), F32)]
    chunk = dil * tps * Q_TILE
    return pl.pallas_call(
        functools.partial(_dilated_kernel, dil=dil, seq=seq, tps=tps, unroll=unroll),
        grid=(B, S // chunk),
        in_specs=in_specs,
        out_specs=[pl.BlockSpec((1, chunk, W), lambda b, i: (b, i, 0)),
                   pl.BlockSpec((1, chunk, 2 * LANES), lambda b, i: (b, i, 0))],
        out_shape=[jax.ShapeDtypeStruct((B, S, W), BF16), jax.ShapeDtypeStruct((B, S, 2 * LANES), F32)],
        scratch_shapes=scratch,
        compiler_params=_params("arbitrary", "arbitrary"),
        name=f"dilated_{dil}",
    )(*args)


def _mla_kernel(q_ref, k_ref, v_ref, o_ref, *, tk, n_sub):
    tq = q_ref.shape[2]
    sub = tq // n_sub
    seq = k_ref.shape[2]
    ones = jnp.ones((tk, B_V), BF16)

    def body(c, carry):
        off = pl.multiple_of(c * tk, tk)
        kc = k_ref[0, 0, pl.ds(off, tk), :]
        vc = jnp.concatenate([v_ref[0, 0, pl.ds(off, tk), :], ones], axis=1)
        new = []
        for j in range(n_sub):
            m, acc = carry[j]
            s = _dot_nt(q_ref[0, 0, j * sub:(j + 1) * sub, :], kc)
            m_new = jnp.maximum(m, jnp.max(s, axis=-1, keepdims=True))
            p = jnp.exp2(s - m_new).astype(BF16)
            acc = jnp.exp2(m - m_new) * acc + _dot(p, vc)
            new.append((m_new, acc))
        return tuple(new)

    init = tuple((jnp.full((sub, 1), -jnp.inf, F32), jnp.zeros((sub, 2 * B_V), F32)) for _ in range(n_sub))
    res = lax.fori_loop(0, seq // tk, body, init, unroll=True)
    for j in range(n_sub):
        acc = res[j][1]
        o_ref[0, j * sub:(j + 1) * sub, :] = (acc[:, :B_V] / acc[:, B_V:]).astype(BF16)


def _mla_attention(qb, kb, vb, tq, tk, n_sub):
    B, H, S, _ = qb.shape
    return pl.pallas_call(
        functools.partial(_mla_kernel, tk=tk, n_sub=n_sub),
        grid=(B, H, S // tq),
        in_specs=[
            pl.BlockSpec((1, 1, tq, B_QK), lambda b, h, i: (b, h, i, 0)),
            pl.BlockSpec((1, 1, S, B_QK), lambda b, h, i: (b, h, 0, 0)),
            pl.BlockSpec((1, 1, S, B_V), lambda b, h, i: (b, h, 0, 0)),
        ],
        out_specs=pl.BlockSpec((1, tq, B_V), lambda b, h, i: (b, i, h)),
        out_shape=jax.ShapeDtypeStruct((B, S, H * B_V), BF16),
        compiler_params=_params("arbitrary", "arbitrary", "arbitrary"),
        name="mla_attn",
    )(qb, kb, vb)


def _mix_out_kernel(*refs):
    nb = len(DILATIONS)
    x_ref = refs[0]
    acc_refs, st_refs = refs[1:1 + nb], refs[1 + nb:1 + 2 * nb]
    ob_ref, mkv_ref, exp_ref, ga_ref, gb_ref, wout_ref, nq_ref, wmq_ref, wmo_ref, o_ref = refs[1 + 2 * nb:]

    m = [st[0, :, :LANES] for st in st_refs]
    l = [st[0, :, LANES:] for st in st_refs]
    m_all = functools.reduce(jnp.maximum, m)
    scale = [jnp.exp2(mb - m_all) for mb in m]
    inv = 1.0 / sum(sb * lb for sb, lb in zip(scale, l))
    oa = 0.0
    for sb, acc_ref in zip(scale, acc_refs):
        w = _dot(jnp.concatenate(_split_bf16(sb * inv), axis=1), exp_ref[...])
        oa = oa + w * acc_ref[0].astype(F32)

    oa = _rms(oa, ga_ref[...]).astype(BF16)
    ob = _rms(ob_ref[0], gb_ref[...]).astype(BF16)
    x1 = x_ref[0] + _dot(oa, wout_ref[0:A_WIDTH, :]) + _dot(ob, wout_ref[A_WIDTH:, :])

    mq = (_dot(_rms(x1, nq_ref[...]).astype(BF16), wmq_ref[...]) * _MQ_SCALE).astype(BF16)
    outs = []
    ones = jnp.ones((N_MEM, M_HEAD_DIM), BF16)
    for hd in range(M_HEADS):
        c = slice(hd * M_HEAD_DIM, (hd + 1) * M_HEAD_DIM)
        s = _dot_nt(mq[:, c], mkv_ref[0, :, c])
        p = jnp.exp2(s - jnp.max(s, axis=-1, keepdims=True)).astype(BF16)
        cv = slice(M_WIDTH + hd * M_HEAD_DIM, M_WIDTH + (hd + 1) * M_HEAD_DIM)
        r = _dot(p, jnp.concatenate([mkv_ref[0, :, cv], ones], axis=1))
        outs.append(r[:, :M_HEAD_DIM] / r[:, M_HEAD_DIM:])
    mo = jnp.concatenate(outs, axis=1).astype(BF16)
    o_ref[0] = x1 + _dot(mo, wmo_ref[...])


def _mix_out(x, branches, ob, mkv, expand, ga, gb, wout, nq, wmq, wmo, tm):
    B, S, D = x.shape
    const = lambda shape: pl.BlockSpec(shape, lambda b, i: (0,) * len(shape))
    tok = lambda n: pl.BlockSpec((1, tm, n), lambda b, i: (b, i, 0))
    accs = [a for a, _ in branches]
    stats = [s for _, s in branches]
    return pl.pallas_call(
        _mix_out_kernel,
        grid=(B, S // tm),
        in_specs=[tok(D)] + [tok(A_WIDTH)] * len(accs) + [tok(2 * LANES)] * len(stats)
        + [tok(B_WIDTH), pl.BlockSpec((1, N_MEM, 2 * M_WIDTH), lambda b, i: (b, 0, 0)),
           const(expand.shape), const((1, A_WIDTH)), const((1, B_WIDTH)), const(wout.shape), const((1, D)),
           const(wmq.shape), const(wmo.shape)],
        out_specs=tok(D),
        out_shape=jax.ShapeDtypeStruct((B, S, D), F32),
        compiler_params=_params("arbitrary", "arbitrary"),
        name="mix_out",
    )(x, *accs, *stats, ob, mkv, expand, ga, gb, wout, nq, wmq, wmo)


def _ffn_kernel(x_ref, nf_ref, wg_ref, wu_ref, wd_ref, nfin_ref, o_ref):
    x = x_ref[...]
    hf = _rms(x, nf_ref[...]).astype(BF16)
    g = _dot(hf, wg_ref[...])
    u = _dot(hf, wu_ref[...])
    a = (g / (1.0 + jnp.exp(-g)) * u).astype(BF16)
    y = x + _dot(a, wd_ref[...])
    o_ref[...] = _rms(y, nfin_ref[...])


def _ffn(x, nf, wg, wu, wd, nfin, tm):
    T, D = x.shape
    const = lambda shape: pl.BlockSpec(shape, lambda i: (0,) * len(shape), pipeline_mode=pl.Buffered(1))
    tok = pl.BlockSpec((tm, D), lambda i: (i, 0))
    return pl.pallas_call(
        _ffn_kernel,
        grid=(T // tm,),
        in_specs=[tok, const((1, D)), const(wg.shape), const(wu.shape), const(wd.shape), const((1, D))],
        out_specs=tok,
        out_shape=jax.ShapeDtypeStruct((T, D), F32),
        compiler_params=_params("arbitrary"),
        name="ffn",
    )(x, nf, wg, wu, wd, nfin)


def _rotate_half_cols(w):
    half = w.shape[-1] // 2
    return jnp.concatenate([-w[..., half:], w[..., :half]], axis=-1)


def kernel(x, mem, positions, norm_mix, w_in, q_norm, w_q_up, kv_norm, w_kv_up, gout_a, gout_b, w_out,
           norm_mem_q, norm_mem_kv, w_mq, w_mkv, w_mo, norm_ffn, w_gate, w_up, w_down, norm_final):
    B, S, D = x.shape
    depth = w_in.shape[0]
    half = B_ROPE // 2
    inv_freq = ROPE_THETA ** (-jnp.arange(half, dtype=F32) / half)
    invf = jnp.tile(inv_freq, LANES // half)[None, :]
    pos_rows = positions.reshape(B, S // LANES, 1, LANES)
    perms = {dil: _class_perm(dil) for dil in _REGROUPED}
    expand = np.zeros((2 * LANES, A_WIDTH), np.float32)
    for hd in range(A_HEADS):
        expand[[hd, LANES + hd], hd * A_HEAD_DIM:(hd + 1) * A_HEAD_DIM] = 1.0

    for l in range(depth):
        w_kr = w_in[l][:, _C_KR:]
        win = jnp.concatenate([w_in[l], _rotate_half_cols(w_kr)], axis=1).astype(BF16)
        wq3 = w_q_up[l].reshape(Q_LORA, B_HEADS, B_QK)
        wq_pe = wq3[:, :, B_NOPE:]
        wq = jnp.concatenate([wq3[:, :, :B_NOPE].reshape(Q_LORA, -1), wq_pe.reshape(Q_LORA, -1),
                              _rotate_half_cols(wq_pe).reshape(Q_LORA, -1)], axis=1).astype(BF16)

        qkv, (qb, kb, vb) = _in_proj(x, pos_rows, invf, norm_mix[l][None], win, q_norm[l][None], wq,
                                     kv_norm[l][None], w_kv_up[l].astype(BF16),
                                     [jnp.asarray(perms[dil], BF16) for dil in _REGROUPED], tm=512)

        branches = [_dilated_branch(*qkv[dil], positions, dil,
                                    None if dil == 1 else jnp.asarray(perms[dil].T, BF16))
                    for dil in DILATIONS]

        ob = _mla_attention(qb, kb, vb, tq=2048, tk=512, n_sub=8)

        mkv = _mem_kv(mem, norm_mem_kv[l][None], w_mkv[l].astype(BF16))
        x = _mix_out(x, branches, ob, mkv, jnp.asarray(expand, BF16), gout_a[l][None], gout_b[l][None],
                     w_out[l].astype(BF16), norm_mem_q[l][None], w_mq[l].astype(BF16), w_mo[l].astype(BF16),
                     tm=512)

        last_norm = norm_final[None] if l == depth - 1 else None
        assert last_norm is not None, "the ffn call applies the final norm; depth is 1 for this problem"
        x = _ffn(x.reshape(B * S, D), norm_ffn[l][None], w_gate[l].astype(BF16), w_up[l].astype(BF16),
                 w_down[l].astype(BF16), last_norm, tm=512).reshape(B, S, D)
    return x
```

```python
import functools
import math

import jax
import jax.numpy as jnp
import numpy as np
from jax import lax
from jax.experimental import pallas as pl
from jax.experimental.pallas import tpu as pltpu

F32 = jnp.float32
BF16 = jnp.bfloat16

EPS = 1e-6
D_MODEL = 1024
N_MEM = 256

A_HEADS = 8
A_HEAD_DIM = 64
A_WIDTH = A_HEADS * A_HEAD_DIM
DILATIONS = (1, 4, 16)
HALF_SPAN = 64
Q_TILE = 128
K_WIN = Q_TILE + 2 * HALF_SPAN
PERM = 256
_DILATED_TILING = {1: (16, 1), 4: (4, 4), 16: (1, 16)}

B_HEADS = 4
B_NOPE = 128
B_ROPE = 64
B_QK = B_NOPE + B_ROPE
B_V = 128
B_WIDTH = B_HEADS * B_V
Q_LORA = 384
KV_LORA = 256
ROPE_THETA = 10000.0

M_HEADS = 4
M_HEAD_DIM = 128
M_WIDTH = M_HEADS * M_HEAD_DIM

LOG2E = math.log2(math.e)
MASK_DIST = 1e30
LANES = 128
VMEM_LIMIT = 56 * 1024 * 1024


def _rms(x, g):
    x = x.astype(F32)
    return x * lax.rsqrt(jnp.mean(x * x, axis=-1, keepdims=True) + EPS) * g


def _dot(a, b):
    return jnp.dot(a, b, preferred_element_type=F32)


def _dot_nt(a, b):
    return lax.dot_general(a, b, (((1,), (1,)), ((), ())), preferred_element_type=F32)


def _split_bf16(x):
    hi = x.astype(BF16)
    return hi, (x - hi.astype(F32)).astype(BF16)


def _row_to_col(row):
    n = row.shape[1]
    diag = lax.broadcasted_iota(jnp.int32, (n, n), 0) == lax.broadcasted_iota(jnp.int32, (n, n), 1)
    return jnp.sum(jnp.where(diag, row, 0), axis=1, keepdims=True)


def _params(*sem):
    return pltpu.CompilerParams(dimension_semantics=sem, vmem_limit_bytes=VMEM_LIMIT)


def _class_perm(dil):
    idx = np.arange(PERM)
    r, a = idx // (PERM // dil), idx % (PERM // dil)
    p = np.zeros((PERM, PERM), np.float32)
    p[idx, dil * a + r] = 1.0
    return p


def _mem_kv_kernel(mem_ref, g_ref, w_ref, o_ref):
    o_ref[0] = _dot(_rms(mem_ref[0], g_ref[...]).astype(BF16), w_ref[...]).astype(BF16)


def _mem_kv(mem, g, w):
    B, M, D = mem.shape
    N = w.shape[1]
    return pl.pallas_call(
        _mem_kv_kernel,
        grid=(B,),
        in_specs=[
            pl.BlockSpec((1, M, D), lambda b: (b, 0, 0)),
            pl.BlockSpec((1, D), lambda b: (0, 0)),
            pl.BlockSpec((D, N), lambda b: (0, 0)),
        ],
        out_specs=pl.BlockSpec((1, M, N), lambda b: (b, 0, 0)),
        out_shape=jax.ShapeDtypeStruct((B, M, N), BF16),
        compiler_params=_params("arbitrary"),
        name="mem_kv",
    )(mem, g, w)


_C_QA, _C_KA, _C_VA, _C_CQ, _C_CKV, _C_KR, _C_END = 0, 512, 1024, 1536, 1920, 2176, 2304
_QA_SCALE = A_HEAD_DIM ** -0.5 * LOG2E
_QB_SCALE = B_QK ** -0.5 * LOG2E
_MQ_SCALE = M_HEAD_DIM ** -0.5 * LOG2E
_REGROUPED = DILATIONS[1:]


def _in_proj_kernel(x_ref, pos_ref, invf_ref, nmix_ref, win_ref, qn_ref, wq_ref, kvn_ref, wkv_ref, *refs, sub_rows):
    perm_refs = refs[:len(_REGROUPED)]
    qa_ref, ka_ref, va_ref, qb_ref, kb_ref, vb_ref = refs[len(_REGROUPED):len(_REGROUPED) + 6]
    class_refs = refs[len(_REGROUPED) + 6:]
    for sub in range(x_ref.shape[1] // sub_rows):
        _in_proj_rows(sub * sub_rows, sub_rows, x_ref, pos_ref, invf_ref, nmix_ref, win_ref, qn_ref, wq_ref, kvn_ref,
                      wkv_ref, perm_refs, (qa_ref, ka_ref, va_ref), (qb_ref, kb_ref, vb_ref), class_refs)


def _in_proj_rows(r0, tm, x_ref, pos_ref, invf_ref, nmix_ref, win_ref, qn_ref, wq_ref, kvn_ref, wkv_ref,
                  perm_refs, nat_refs, mla_refs, class_refs):
    rows = slice(r0, r0 + tm)
    qb_ref, kb_ref, vb_ref = mla_refs
    h = _rms(x_ref[0, rows, :], nmix_ref[...]).astype(BF16)

    def seg(a, b):
        return _dot(h, win_ref[:, a:b])

    qkv = [(seg(_C_QA, _C_KA) * _QA_SCALE).astype(BF16), seg(_C_KA, _C_VA).astype(BF16),
           seg(_C_VA, _C_CQ).astype(BF16)]
    for val, ref in zip(qkv, nat_refs):
        ref[0, 0, rows, :] = val
    for n, dil in enumerate(_REGROUPED):
        per = PERM // dil
        for part in range(tm // PERM):
            c0 = r0 // dil + part * per
            for val, ref in zip(qkv, class_refs[3 * n:3 * n + 3]):
                y = _dot(perm_refs[n][...], val[part * PERM:(part + 1) * PERM])
                ref[0, :, c0:c0 + per, :] = y.reshape(dil, per, A_WIDTH).astype(BF16)

    cq = seg(_C_CQ, _C_CKV)
    ckv = seg(_C_CKV, _C_KR)
    kr2 = seg(_C_KR, _C_END)

    pos = jnp.concatenate([_row_to_col(pos_ref[0, r0 // LANES + g]) for g in range(tm // LANES)], axis=0)
    ang = pos.astype(F32) * invf_ref[...]
    cos4 = jnp.cos(ang)
    sin4 = jnp.sin(ang)
    cos8 = jnp.concatenate([cos4, cos4], axis=1)
    sin8 = jnp.concatenate([sin4, sin4], axis=1)

    qb = _dot(_rms(cq, qn_ref[...]).astype(BF16), wq_ref[...])
    n0 = B_HEADS * B_NOPE
    n1 = n0 + B_HEADS * B_ROPE
    q_pe = (qb[:, n0:n1] * cos8 + qb[:, n1:] * sin8) * _QB_SCALE

    lane = lax.broadcasted_iota(jnp.int32, kr2.shape, 1)
    t = kr2 * jnp.where(lane < B_ROPE, cos4, sin4)
    k_pe = (t[:, :B_ROPE] + t[:, B_ROPE:]).astype(BF16)

    kvb = _dot(_rms(ckv, kvn_ref[...]).astype(BF16), wkv_ref[...])
    for hd in range(B_HEADS):
        qb_ref[0, hd, rows, 0:B_NOPE] = (qb[:, hd * B_NOPE:(hd + 1) * B_NOPE] * _QB_SCALE).astype(BF16)
        qb_ref[0, hd, rows, B_NOPE:B_QK] = q_pe[:, hd * B_ROPE:(hd + 1) * B_ROPE].astype(BF16)
        c0 = hd * (B_NOPE + B_V)
        kb_ref[0, hd, rows, 0:B_NOPE] = kvb[:, c0:c0 + B_NOPE].astype(BF16)
        kb_ref[0, hd, rows, B_NOPE:B_QK] = k_pe
        vb_ref[0, hd, rows, :] = kvb[:, c0 + B_NOPE:c0 + B_NOPE + B_V].astype(BF16)


def _in_proj(x, pos_rows, invf, nmix, win, qn, wq, kvn, wkv, perms, tm, sub_rows):
    B, S, D = x.shape
    const = lambda shape: pl.BlockSpec(shape, lambda b, i: (0,) * len(shape))
    tok = lambda n: pl.BlockSpec((1, tm, n), lambda b, i: (b, i, 0))
    head = lambda n: pl.BlockSpec((1, B_HEADS, tm, n), lambda b, i: (b, 0, i, 0))
    cls = lambda dil: pl.BlockSpec((1, dil, tm // dil, A_WIDTH), lambda b, i: (b, 0, i, 0))
    out_specs = [cls(1)] * 3 + [head(B_QK), head(B_QK), head(B_V)]
    out_shape = ([jax.ShapeDtypeStruct((B, 1, S, A_WIDTH), BF16)] * 3
                 + [jax.ShapeDtypeStruct((B, B_HEADS, S, B_QK), BF16)] * 2
                 + [jax.ShapeDtypeStruct((B, B_HEADS, S, B_V), BF16)])
    for dil in _REGROUPED:
        out_specs += [cls(dil)] * 3
        out_shape += [jax.ShapeDtypeStruct((B, dil, S // dil, A_WIDTH), BF16)] * 3
    res = pl.pallas_call(
        functools.partial(_in_proj_kernel, sub_rows=sub_rows),
        grid=(B, S // tm),
        in_specs=[tok(D), pl.BlockSpec((1, tm // LANES, 1, LANES), lambda b, i: (b, i, 0, 0)),
                  const((1, LANES)), const((1, D)), const(win.shape),
                  const((1, Q_LORA)), const(wq.shape), const((1, KV_LORA)), const(wkv.shape)]
        + [const((PERM, PERM))] * len(perms),
        out_specs=out_specs,
        out_shape=out_shape,
        compiler_params=_params("arbitrary", "arbitrary"),
        name="in_proj",
    )(x, pos_rows, invf, nmix, win, qn, wq, kvn, wkv, *perms)
    qkv = {1: tuple(res[0:3])}
    for n, dil in enumerate(_REGROUPED):
        qkv[dil] = tuple(res[6 + 3 * n:9 + 3 * n])
    return qkv, res[3:6]


def _tile_attention(q, kw, vw, dist):
    lane = lax.broadcasted_iota(jnp.int32, (Q_TILE, LANES), 1)
    low = lane < A_HEAD_DIM
    ones = jnp.ones((K_WIN, LANES), BF16)
    m_tile = jnp.zeros((Q_TILE, LANES), F32)
    l_tile = jnp.ones((Q_TILE, LANES), F32)
    pairs = []
    for pair in range(A_HEADS // 2):
        c = slice(pair * LANES, (pair + 1) * LANES)
        qp, kp = q[:, c], kw[:, c]
        v_ext = jnp.concatenate([vw[:, c], ones], axis=1)
        acc = []
        for half in range(2):
            hd = 2 * pair + half
            qm = jnp.where(low if half == 0 else ~low, qp, jnp.zeros_like(qp))
            s = _dot_nt(qm, kp) - (2.0 ** -(hd + 1) * LOG2E) * dist
            m = jnp.max(s, axis=-1, keepdims=True)
            r = _dot(jnp.exp2(s - m).astype(BF16), v_ext)
            acc.append(r[:, :LANES])
            m_tile = jnp.where(lane == hd, m, m_tile)
            l_tile = jnp.where(lane == hd, r[:, LANES:], l_tile)
        pairs.append(jnp.where(low, acc[0], acc[1]))
    return jnp.concatenate(pairs, axis=1), jnp.concatenate([m_tile, l_tile], axis=1)


def _dilated_kernel(*refs, dil, seq, tps, unroll):
    if dil == 1:
        q_ref, k_ref, v_ref, pq_ref, pk_ref, acc_ref, st_ref = refs
    else:
        q_ref, k_ref, v_ref, pq_ref, pk_ref, pt_ref, acc_ref, st_ref, acc_scr, st_scr = refs
    step = pl.program_id(1)

    def tile(cls, t_local):
        start = (step * tps + t_local) * Q_TILE
        if seq == K_WIN:
            win0 = 0
        else:
            win0 = pl.multiple_of(jnp.clip(start - HALF_SPAN, 0, seq - K_WIN), HALF_SPAN)
        rows = pl.ds(t_local * Q_TILE, Q_TILE)
        dist = jnp.abs(_row_to_col(pq_ref[0, cls, t_local]) - pk_ref[0, t_local, cls]).astype(F32)
        qi = start + lax.broadcasted_iota(jnp.int32, (Q_TILE, 1), 0)
        ki = win0 + lax.broadcasted_iota(jnp.int32, (1, K_WIN), 1)
        dist = jnp.where(jnp.abs(ki - qi) <= HALF_SPAN, dist, MASK_DIST)
        return _tile_attention(q_ref[0, cls, rows, :], k_ref[0, cls, pl.ds(win0, K_WIN), :],
                               v_ref[0, cls, pl.ds(win0, K_WIN), :], dist)

    if dil == 1:
        for t in range(tps):
            acc, st = tile(0, t)
            acc_ref[0, t * Q_TILE:(t + 1) * Q_TILE, :] = acc.astype(BF16)
            st_ref[0, t * Q_TILE:(t + 1) * Q_TILE, :] = st
        return

    def body(cls, carry):
        for t in range(tps):
            acc, st = tile(cls, t)
            acc_scr[cls, t * Q_TILE:(t + 1) * Q_TILE, :] = acc.astype(BF16)
            st_scr[cls, t * Q_TILE:(t + 1) * Q_TILE, :] = st
        return carry

    if unroll >= dil:
        for cls in range(dil):
            body(cls, 0)
    else:
        lax.fori_loop(0, dil, body, 0, unroll=unroll)

    per = PERM // dil
    for part in range(dil * tps * Q_TILE // PERM):
        rows = slice(part * per, (part + 1) * per)
        out = slice(part * PERM, (part + 1) * PERM)
        acc = jnp.concatenate([acc_scr[cls, rows, :] for cls in range(dil)], axis=0)
        acc_ref[0, out, :] = _dot(pt_ref[...], acc).astype(BF16)
        hi, lo = _split_bf16(jnp.concatenate([st_scr[cls, rows, :] for cls in range(dil)], axis=0))
        st_ref[0, out, :] = _dot(pt_ref[...], hi) + _dot(pt_ref[...], lo)


def _dilated_branch(q, k, v, positions, dil, perm_t):
    B, _, seq, W = q.shape
    S = seq * dil
    nblk = seq // Q_TILE
    tps, unroll = _DILATED_TILING[dil]
    pos_cls = positions.reshape(B, seq, dil).transpose(0, 2, 1)
    pos_q = pos_cls.reshape(B, dil, nblk, 1, Q_TILE)
    starts = [min(max(i * Q_TILE - HALF_SPAN, 0), seq - K_WIN) for i in range(nblk)]
    pos_k = jnp.stack([pos_cls[:, :, s:s + K_WIN] for s in starts], axis=1)[:, :, :, None, :]

    whole = pl.BlockSpec((1, dil, seq, W), lambda b, i: (b, 0, 0, 0))
    in_specs = [pl.BlockSpec((1, dil, tps * Q_TILE, W), lambda b, i: (b, 0, i, 0)), whole, whole,
                pl.BlockSpec((1, dil, tps, 1, Q_TILE), lambda b, i: (b, 0, i, 0, 0)),
                pl.BlockSpec((1, tps, dil, 1, K_WIN), lambda b, i: (b, i, 0, 0, 0))]
    args = [q, k, v, pos_q, pos_k]
    scratch = []
    if dil > 1:
        in_specs.append(pl.BlockSpec((PERM, PERM), lambda b, i: (0, 0)))
        args.append(perm_t)
        scratch = [pltpu.VMEM((dil, tps * Q_TILE, W), BF16), pltpu.VMEM((dil, tps * Q_TILE, 2 * LANES), F32)]
    chunk = dil * tps * Q_TILE
    return pl.pallas_call(
        functools.partial(_dilated_kernel, dil=dil, seq=seq, tps=tps, unroll=unroll),
        grid=(B, S // chunk),
        in_specs=in_specs,
        out_specs=[pl.BlockSpec((1, chunk, W), lambda b, i: (b, i, 0)),
                   pl.BlockSpec((1, chunk, 2 * LANES), lambda b, i: (b, i, 0))],
        out_shape=[jax.ShapeDtypeStruct((B, S, W), BF16), jax.ShapeDtypeStruct((B, S, 2 * LANES), F32)],
        scratch_shapes=scratch,
        compiler_params=_params("arbitrary", "arbitrary"),
        name=f"dilated_{dil}",
    )(*args)


def _mla_kernel(q_ref, k_ref, v_ref, o_ref, *, tk, n_sub):
    tq = q_ref.shape[2]
    sub = tq // n_sub
    seq = k_ref.shape[2]
    ones = jnp.ones((tk, B_V), BF16)

    def body(c, carry):
        off = pl.multiple_of(c * tk, tk)
        kc = k_ref[0, 0, pl.ds(off, tk), :]
        vc = jnp.concatenate([v_ref[0, 0, pl.ds(off, tk), :], ones], axis=1)
        new = []
        for j in range(n_sub):
            m, acc = carry[j]
            s = _dot_nt(q_ref[0, 0, j * sub:(j + 1) * sub, :], kc)
            m_new = jnp.maximum(m, jnp.max(s, axis=-1, keepdims=True))
            p = jnp.exp2(s - m_new).astype(BF16)
            acc = jnp.exp2(m - m_new) * acc + _dot(p, vc)
            new.append((m_new, acc))
        return tuple(new)

    init = tuple((jnp.full((sub, 1), -jnp.inf, F32), jnp.zeros((sub, 2 * B_V), F32)) for _ in range(n_sub))
    res = lax.fori_loop(0, seq // tk, body, init, unroll=True)
    for j in range(n_sub):
        acc = res[j][1]
        o_ref[0, j * sub:(j + 1) * sub, :] = (acc[:, :B_V] / acc[:, B_V:]).astype(BF16)


def _mla_attention(qb, kb, vb, tq, tk, n_sub):
    B, H, S, _ = qb.shape
    return pl.pallas_call(
        functools.partial(_mla_kernel, tk=tk, n_sub=n_sub),
        grid=(B, H, S // tq),
        in_specs=[
            pl.BlockSpec((1, 1, tq, B_QK), lambda b, h, i: (b, h, i, 0)),
            pl.BlockSpec((1, 1, S, B_QK), lambda b, h, i: (b, h, 0, 0)),
            pl.BlockSpec((1, 1, S, B_V), lambda b, h, i: (b, h, 0, 0)),
        ],
        out_specs=pl.BlockSpec((1, tq, B_V), lambda b, h, i: (b, i, h)),
        out_shape=jax.ShapeDtypeStruct((B, S, H * B_V), BF16),
        compiler_params=_params("arbitrary", "arbitrary", "arbitrary"),
        name="mla_attn",
    )(qb, kb, vb)


def _mix_out_kernel(*refs, sub_rows):
    for sub in range(refs[0].shape[1] // sub_rows):
        _mix_out_rows(slice(sub * sub_rows, (sub + 1) * sub_rows), *refs)


def _mix_out_rows(rows, *refs):
    nb = len(DILATIONS)
    x_ref = refs[0]
    acc_refs, st_refs = refs[1:1 + nb], refs[1 + nb:1 + 2 * nb]
    ob_ref, mkv_ref, exp_ref, ga_ref, gb_ref, wout_ref, nq_ref, wmq_ref, wmo_ref, o_ref = refs[1 + 2 * nb:]

    m = [st[0, rows, :LANES] for st in st_refs]
    l = [st[0, rows, LANES:] for st in st_refs]
    m_all = functools.reduce(jnp.maximum, m)
    scale = [jnp.exp2(mb - m_all) for mb in m]
    inv = 1.0 / sum(sb * lb for sb, lb in zip(scale, l))
    oa = 0.0
    for sb, acc_ref in zip(scale, acc_refs):
        w = _dot(jnp.concatenate(_split_bf16(sb * inv), axis=1), exp_ref[...])
        oa = oa + w * acc_ref[0, rows, :].astype(F32)

    oa = _rms(oa, ga_ref[...]).astype(BF16)
    ob = _rms(ob_ref[0, rows, :], gb_ref[...]).astype(BF16)
    x1 = x_ref[0, rows, :] + _dot(oa, wout_ref[0:A_WIDTH, :]) + _dot(ob, wout_ref[A_WIDTH:, :])

    mq = (_dot(_rms(x1, nq_ref[...]).astype(BF16), wmq_ref[...]) * _MQ_SCALE).astype(BF16)
    outs = []
    ones = jnp.ones((N_MEM, M_HEAD_DIM), BF16)
    for hd in range(M_HEADS):
        c = slice(hd * M_HEAD_DIM, (hd + 1) * M_HEAD_DIM)
        s = _dot_nt(mq[:, c], mkv_ref[0, :, c])
        p = jnp.exp2(s - jnp.max(s, axis=-1, keepdims=True)).astype(BF16)
        cv = slice(M_WIDTH + hd * M_HEAD_DIM, M_WIDTH + (hd + 1) * M_HEAD_DIM)
        r = _dot(p, jnp.concatenate([mkv_ref[0, :, cv], ones], axis=1))
        outs.append(r[:, :M_HEAD_DIM] / r[:, M_HEAD_DIM:])
    mo = jnp.concatenate(outs, axis=1).astype(BF16)
    o_ref[0, rows, :] = x1 + _dot(mo, wmo_ref[...])


def _mix_out(x, branches, ob, mkv, expand, ga, gb, wout, nq, wmq, wmo, tm, sub_rows):
    B, S, D = x.shape
    const = lambda shape: pl.BlockSpec(shape, lambda b, i: (0,) * len(shape))
    tok = lambda n: pl.BlockSpec((1, tm, n), lambda b, i: (b, i, 0))
    accs = [a for a, _ in branches]
    stats = [s for _, s in branches]
    return pl.pallas_call(
        functools.partial(_mix_out_kernel, sub_rows=sub_rows),
        grid=(B, S // tm),
        in_specs=[tok(D)] + [tok(A_WIDTH)] * len(accs) + [tok(2 * LANES)] * len(stats)
        + [tok(B_WIDTH), pl.BlockSpec((1, N_MEM, 2 * M_WIDTH), lambda b, i: (b, 0, 0)),
           const(expand.shape), const((1, A_WIDTH)), const((1, B_WIDTH)), const(wout.shape), const((1, D)),
           const(wmq.shape), const(wmo.shape)],
        out_specs=tok(D),
        out_shape=jax.ShapeDtypeStruct((B, S, D), F32),
        compiler_params=_params("arbitrary", "arbitrary"),
        name="mix_out",
    )(x, *accs, *stats, ob, mkv, expand, ga, gb, wout, nq, wmq, wmo)


def _ffn_kernel(x_ref, nf_ref, wg_ref, wu_ref, wd_ref, nfin_ref, o_ref):
    x = x_ref[...]
    hf = _rms(x, nf_ref[...]).astype(BF16)
    g = _dot(hf, wg_ref[...])
    u = _dot(hf, wu_ref[...])
    a = (g / (1.0 + jnp.exp(-g)) * u).astype(BF16)
    y = x + _dot(a, wd_ref[...])
    o_ref[...] = _rms(y, nfin_ref[...])


def _ffn(x, nf, wg, wu, wd, nfin, tm):
    T, D = x.shape
    const = lambda shape: pl.BlockSpec(shape, lambda i: (0,) * len(shape), pipeline_mode=pl.Buffered(1))
    tok = pl.BlockSpec((tm, D), lambda i: (i, 0))
    return pl.pallas_call(
        _ffn_kernel,
        grid=(T // tm,),
        in_specs=[tok, const((1, D)), const(wg.shape), const(wu.shape), const(wd.shape), const((1, D))],
        out_specs=tok,
        out_shape=jax.ShapeDtypeStruct((T, D), F32),
        compiler_params=_params("arbitrary"),
        name="ffn",
    )(x, nf, wg, wu, wd, nfin)


def _rotate_half_cols(w):
    half = w.shape[-1] // 2
    return jnp.concatenate([-w[..., half:], w[..., :half]], axis=-1)


def kernel(x, mem, positions, norm_mix, w_in, q_norm, w_q_up, kv_norm, w_kv_up, gout_a, gout_b, w_out,
           norm_mem_q, norm_mem_kv, w_mq, w_mkv, w_mo, norm_ffn, w_gate, w_up, w_down, norm_final):
    B, S, D = x.shape
    depth = w_in.shape[0]
    half = B_ROPE // 2
    inv_freq = ROPE_THETA ** (-jnp.arange(half, dtype=F32) / half)
    invf = jnp.tile(inv_freq, LANES // half)[None, :]
    pos_rows = positions.reshape(B, S // LANES, 1, LANES)
    perms = {dil: _class_perm(dil) for dil in _REGROUPED}
    expand = np.zeros((2 * LANES, A_WIDTH), np.float32)
    for hd in range(A_HEADS):
        expand[[hd, LANES + hd], hd * A_HEAD_DIM:(hd + 1) * A_HEAD_DIM] = 1.0

    for l in range(depth):
        w_kr = w_in[l][:, _C_KR:]
        win = jnp.concatenate([w_in[l], _rotate_half_cols(w_kr)], axis=1).astype(BF16)
        wq3 = w_q_up[l].reshape(Q_LORA, B_HEADS, B_QK)
        wq_pe = wq3[:, :, B_NOPE:]
        wq = jnp.concatenate([wq3[:, :, :B_NOPE].reshape(Q_LORA, -1), wq_pe.reshape(Q_LORA, -1),
                              _rotate_half_cols(wq_pe).reshape(Q_LORA, -1)], axis=1).astype(BF16)

        qkv, (qb, kb, vb) = _in_proj(x, pos_rows, invf, norm_mix[l][None], win, q_norm[l][None], wq,
                                     kv_norm[l][None], w_kv_up[l].astype(BF16),
                                     [jnp.asarray(perms[dil], BF16) for dil in _REGROUPED], tm=1024, sub_rows=512)

        branches = [_dilated_branch(*qkv[dil], positions, dil,
                                    None if dil == 1 else jnp.asarray(perms[dil].T, BF16))
                    for dil in DILATIONS]

        ob = _mla_attention(qb, kb, vb, tq=2048, tk=512, n_sub=8)

        mkv = _mem_kv(mem, norm_mem_kv[l][None], w_mkv[l].astype(BF16))
        x = _mix_out(x, branches, ob, mkv, jnp.asarray(expand, BF16), gout_a[l][None], gout_b[l][None],
                     w_out[l].astype(BF16), norm_mem_q[l][None], w_mq[l].astype(BF16), w_mo[l].astype(BF16),
                     tm=1024, sub_rows=512)

        last_norm = norm_final[None] if l == depth - 1 else None
        assert last_norm is not None, "the ffn call applies the final norm; depth is 1 for this problem"
        x = _ffn(x.reshape(B * S, D), norm_ffn[l][None], w_gate[l].astype(BF16), w_up[l].astype(BF16),
                 w_down[l].astype(BF16), last_norm, tm=512).reshape(B, S, D)
    return x
```
